```python
import math
import jax, jax.numpy as jnp
from jax import lax
import numpy as np


D_MODEL = 1024
BATCH = 1
SEQ = 16384
DEPTH = 1
DEC_BATCH = 16
DEC_SEQ = 32
PAST_LEN = 4096

CHUNK = 64
Q_BLOCK = 128
N_MEM = 256
EPS = 1e-6
NEG_INF = -1e30
MLA_HEADS = 8
MLA_Q_RANK = 384
MLA_KV_RANK = 256
MLA_NOPE = 64
MLA_ROPE = 32
MLA_V = 64
MLA_THETA = 10000.0
MLA_SCALE = (MLA_NOPE + MLA_ROPE) ** -0.5
DIFF_HEADS = 8
DIFF_DC = 32
DIFF_V = 2 * DIFF_DC
DIFF_ROT = DIFF_DC // 4
ROPE_THETA = 500000.0
DIFF_SCALE = DIFF_DC ** -0.5
MEM_HEADS = 4
MEM_DH = 128
MEM_SCALE = MEM_DH ** -0.5
D_FF = 4 * D_MODEL
N_BRANCH = 3
DIFF_QK_W = DIFF_HEADS * 2 * DIFF_DC
DIFF_V_W = DIFF_HEADS * DIFF_V
MEM_W = MEM_HEADS * MEM_DH
MLA_O_W = MLA_HEADS * MLA_V
IN_SIZES = (MLA_Q_RANK, MLA_KV_RANK, MLA_ROPE, DIFF_QK_W, DIFF_QK_W, DIFF_V_W, MEM_W)
IN_SPLIT_POINTS = tuple(int(v) for v in np.cumsum(IN_SIZES)[:-1])
IN_WIDTH = int(sum(IN_SIZES))

kernel_name = "hybrid_mla_diffattn_memory_stream_step"


def rmsnorm(x, g):
    xf = x.astype(jnp.float32)
    y = xf * lax.rsqrt(jnp.mean(xf * xf, axis=-1, keepdims=True) + EPS)
    return (y * g.astype(jnp.float32)).astype(x.dtype)


def rope(x, pos, rot_dim, theta):
    half = rot_dim // 2
    inv = jnp.power(jnp.float32(theta), -jnp.arange(half, dtype=jnp.float32) * (2.0 / rot_dim))
    ang = pos.astype(jnp.float32)[:, None] * inv[None, :]
    shape = (1, pos.shape[0]) + (1,) * (x.ndim - 3) + (half,)
    c = jnp.cos(ang).reshape(shape)
    s = jnp.sin(ang).reshape(shape)
    xf = x.astype(jnp.float32)
    x1 = xf[..., :half]
    x2 = xf[..., half:rot_dim]
    out = jnp.concatenate([x1 * c - x2 * s, x2 * c + x1 * s, xf[..., rot_dim:]], axis=-1)
    return out.astype(x.dtype)


def sweep_queries(fn, q_args, q_pos):
    t = q_pos.shape[0]
    if t <= Q_BLOCK or t % Q_BLOCK != 0:
        return fn(q_args, q_pos)
    nb = t // Q_BLOCK

    def to_blocks(a):
        return jnp.moveaxis(a.reshape((a.shape[0], nb, Q_BLOCK) + a.shape[2:]), 1, 0)

    def from_blocks(o):
        return jnp.moveaxis(o, 0, 1).reshape((o.shape[1], t) + o.shape[3:])

    blocks = jax.tree_util.tree_map(to_blocks, q_args)
    outs = lax.map(lambda bp: fn(bp[0], bp[1]), (blocks, q_pos.reshape(nb, Q_BLOCK)))
    return jax.tree_util.tree_map(from_blocks, outs)


def memory_kv(mem, norm_g, w_k, w_v):
    b, m, _ = mem.shape
    mn = rmsnorm(mem, norm_g)
    k = jnp.einsum('bmd,de->bme', mn, w_k).reshape(b, m, MEM_HEADS, MEM_DH)
    v = jnp.einsum('bmd,de->bme', mn, w_v).reshape(b, m, MEM_HEADS, MEM_DH)
    return k, v


def layer_forward(x, pos, past, mem_k, mem_v, p, lam_init):
    b, t, _ = x.shape
    f32 = jnp.float32
    xn = rmsnorm(x, p['pre_mix_g'])
    proj = jnp.einsum('btd,de->bte', xn, p['w_in'])
    cq, ckv, kr, dq, dk, dv, mq = jnp.split(proj, IN_SPLIT_POINTS, axis=-1)

    q = jnp.einsum('btr,re->bte', rmsnorm(cq, p['mla_q_norm_g']), p['mla_w_uq'])
    q = q.reshape(b, t, MLA_HEADS, MLA_NOPE + MLA_ROPE)
    q_nope = q[..., :MLA_NOPE]
    q_rope = rope(q[..., MLA_NOPE:], pos, MLA_ROPE, MLA_THETA)
    ckv_new = rmsnorm(ckv, p['mla_kv_norm_g'])
    kr_new = rope(kr[:, :, None, :], pos, MLA_ROPE, MLA_THETA)[:, :, 0, :]

    dq = rope(dq.reshape(b, t, DIFF_HEADS, 2, DIFF_DC), pos, DIFF_ROT, ROPE_THETA)
    dk_new = rope(dk.reshape(b, t, DIFF_HEADS, 2, DIFF_DC), pos, DIFF_ROT, ROPE_THETA)
    dk_new = dk_new.reshape(b, t, DIFF_HEADS, DIFF_V)
    dv_new = dv.reshape(b, t, DIFF_HEADS, DIFF_V)

    mq = mq.reshape(b, t, MEM_HEADS, MEM_DH)

    if past is None:
        ckv_all, kr_all, dk_all, dv_all = ckv_new, kr_new, dk_new, dv_new
    else:
        ckv_all = jnp.concatenate([past[0], ckv_new], axis=1)
        kr_all = jnp.concatenate([past[1], kr_new], axis=1)
        dk_all = jnp.concatenate([past[2], dk_new], axis=1)
        dv_all = jnp.concatenate([past[3], dv_new], axis=1)
    s_len = ckv_all.shape[1]
    k_chunk = jnp.arange(s_len, dtype=jnp.int32) // CHUNK
    k_nope = jnp.einsum('bsr,rhn->bshn', ckv_all, p['mla_w_uk'])
    v_mla = jnp.einsum('bsr,rhv->bshv', ckv_all, p['mla_w_uv'])
    dk_all = dk_all.reshape(b, s_len, DIFF_HEADS, 2, DIFF_DC)
    lam = (jnp.exp(jnp.sum(p['diff_lq1'].astype(f32) * p['diff_lk1'].astype(f32)))
           - jnp.exp(jnp.sum(p['diff_lq2'].astype(f32) * p['diff_lk2'].astype(f32)))
           + lam_init)

    def block(qa, q_pos):
        qn, qr, qd, qm = qa
        mask = k_chunk[None, :] <= (q_pos // CHUNK)[:, None]
        s = (jnp.einsum('bthn,bshn->bhts', qn, k_nope)
             + jnp.einsum('bthe,bse->bhts', qr, kr_all)).astype(f32) * MLA_SCALE
        pm = jax.nn.softmax(jnp.where(mask, s, NEG_INF), axis=-1).astype(v_mla.dtype)
        o_a = jnp.einsum('bhts,bshv->bthv', pm, v_mla)
        sd = jnp.einsum('bthcd,bshcd->bchts', qd, dk_all).astype(f32) * DIFF_SCALE
        pd = jax.nn.softmax(jnp.where(mask, sd, NEG_INF), axis=-1)
        a = (pd[:, 0] - lam * pd[:, 1]).astype(dv_all.dtype)
        o_b = jnp.einsum('bhts,bshv->bthv', a, dv_all)
        sm = jnp.einsum('bthd,bmhd->bhtm', qm, mem_k).astype(f32) * MEM_SCALE
        pmem = jax.nn.softmax(sm, axis=-1).astype(mem_v.dtype)
        o_c = jnp.einsum('bhtm,bmhd->bthd', pmem, mem_v)
        return (o_a, o_b, o_c)

    o_a, o_b, o_c = sweep_queries(block, (q_nope, q_rope, dq, mq), pos)
    o_mla = o_a.reshape(b, t, MLA_O_W)
    o_diff = (rmsnorm(o_b, p['diff_subln_g']) * (1.0 - lam_init)).reshape(b, t, DIFF_V_W)
    o_mem = o_c.reshape(b, t, MEM_W)

    gates = jax.nn.sigmoid(jnp.einsum('btd,de->bte', xn, p['w_gate']) + p['b_gate'])
    gates = gates.reshape(b, t, N_BRANCH, D_MODEL)
    merged = (gates[:, :, 0] * jnp.einsum('bte,ed->btd', o_mla, p['w_o_mla'])
              + gates[:, :, 1] * jnp.einsum('bte,ed->btd', o_diff, p['w_o_diff'])
              + gates[:, :, 2] * jnp.einsum('bte,ed->btd', o_mem, p['w_o_mem']))
    mix = jnp.einsum('btd,de->bte', merged, p['w_out'])
    x = x + rmsnorm(mix, p['post_mix_g'])

    h = rmsnorm(x, p['pre_mlp_g'])
    u = jax.nn.relu(jnp.einsum('btd,df->btf', h, p['w_mlp_up']))
    f = jnp.einsum('btf,fd->btd', u * u, p['w_mlp_down'])
    x = x + rmsnorm(f, p['post_mlp_g'])
    return x, (ckv_new, kr_new, dk_new, dv_new)


def setup_inputs(seed: int = 0) -> dict:
    key = jax.random.key(seed)
    keys = list(jax.random.split(key, 40))

    def nrm(shape, scale=1.0):
        return scale * jax.random.normal(keys.pop(), shape, dtype=jnp.float32)

    def gain(n):
        return 1.0 + 0.05 * nrm((DEPTH, n))

    d = D_MODEL
    return {
        'x_prompt': nrm((BATCH, SEQ, d)),
        'x_sample': nrm((DEC_BATCH, DEC_SEQ, d)),
        'cache_mla_ckv': nrm((DEPTH, DEC_BATCH, PAST_LEN, MLA_KV_RANK)),
        'cache_mla_krope': nrm((DEPTH, DEC_BATCH, PAST_LEN, MLA_ROPE)),
        'cache_diff_k': nrm((DEPTH, DEC_BATCH, PAST_LEN, DIFF_HEADS, DIFF_V)),
        'cache_diff_v': nrm((DEPTH, DEC_BATCH, PAST_LEN, DIFF_HEADS, DIFF_V)),
        'cache_mem_k': nrm((DEPTH, DEC_BATCH, N_MEM, MEM_HEADS, MEM_DH)),
        'cache_mem_v': nrm((DEPTH, DEC_BATCH, N_MEM, MEM_HEADS, MEM_DH)),
        'mem_prompt': nrm((BATCH, N_MEM, d)),
        'pre_mix_g': gain(d),
        'w_in': nrm((DEPTH, d, IN_WIDTH), d ** -0.5),
        'mla_q_norm_g': gain(MLA_Q_RANK),
        'mla_w_uq': nrm((DEPTH, MLA_Q_RANK, MLA_HEADS * (MLA_NOPE + MLA_ROPE)), MLA_Q_RANK ** -0.5),
        'mla_kv_norm_g': gain(MLA_KV_RANK),
        'mla_w_uk': nrm((DEPTH, MLA_KV_RANK, MLA_HEADS, MLA_NOPE), MLA_KV_RANK ** -0.5),
        'mla_w_uv': nrm((DEPTH, MLA_KV_RANK, MLA_HEADS, MLA_V), MLA_KV_RANK ** -0.5),
        'diff_lq1': nrm((DEPTH, DIFF_DC), 0.1),
        'diff_lk1': nrm((DEPTH, DIFF_DC), 0.1),
        'diff_lq2': nrm((DEPTH, DIFF_DC), 0.1),
        'diff_lk2': nrm((DEPTH, DIFF_DC), 0.1),
        'diff_subln_g': gain(DIFF_V),
        'mem_norm_g': gain(d),
        'w_mem_k': nrm((DEPTH, d, MEM_W), d ** -0.5),
        'w_mem_v': nrm((DEPTH, d, MEM_W), d ** -0.5),
        'w_o_mla': nrm((DEPTH, MLA_O_W, d), MLA_O_W ** -0.5),
        'w_o_diff': nrm((DEPTH, DIFF_V_W, d), DIFF_V_W ** -0.5),
        'w_o_mem': nrm((DEPTH, MEM_W, d), MEM_W ** -0.5),
        'w_gate': nrm((DEPTH, d, N_BRANCH * d), d ** -0.5),
        'b_gate': nrm((DEPTH, N_BRANCH * d), 0.01),
        'w_out': nrm((DEPTH, d, d), d ** -0.5),
        'post_mix_g': gain(d),
        'pre_mlp_g': gain(d),
        'w_mlp_up': nrm((DEPTH, d, D_FF), d ** -0.5),
        'w_mlp_down': nrm((DEPTH, D_FF, d), D_FF ** -0.5),
        'post_mlp_g': gain(d),
    }


def reference(x_prompt, x_sample, cache_mla_ckv, cache_mla_krope, cache_diff_k, cache_diff_v,
              cache_mem_k, cache_mem_v, mem_prompt, pre_mix_g, w_in, mla_q_norm_g, mla_w_uq,
              mla_kv_norm_g, mla_w_uk, mla_w_uv, diff_lq1, diff_lk1, diff_lq2, diff_lk2,
              diff_subln_g, mem_norm_g, w_mem_k, w_mem_v, w_o_mla, w_o_diff, w_o_mem, w_gate,
              b_gate, w_out, post_mix_g, pre_mlp_g, w_mlp_up, w_mlp_down, post_mlp_g):
    past_len = cache_mla_ckv.shape[2]
    pos_p = jnp.arange(x_prompt.shape[1], dtype=jnp.int32)
    pos_s = past_len + jnp.arange(x_sample.shape[1], dtype=jnp.int32)
    xp, xs = x_prompt, x_sample
    p_ckv, p_kr, p_dk, p_dv, p_mk, p_mv = [], [], [], [], [], []
    s_ckv, s_kr, s_dk, s_dv = [], [], [], []
    for l in range(DEPTH):
        lam_init = 0.8 - 0.6 * math.exp(-0.3 * l)
        p = {
            'pre_mix_g': pre_mix_g[l], 'w_in': w_in[l],
            'mla_q_norm_g': mla_q_norm_g[l], 'mla_w_uq': mla_w_uq[l],
            'mla_kv_norm_g': mla_kv_norm_g[l], 'mla_w_uk': mla_w_uk[l], 'mla_w_uv': mla_w_uv[l],
            'diff_lq1': diff_lq1[l], 'diff_lk1': diff_lk1[l],
            'diff_lq2': diff_lq2[l], 'diff_lk2': diff_lk2[l], 'diff_subln_g': diff_subln_g[l],
            'w_o_mla': w_o_mla[l], 'w_o_diff': w_o_diff[l], 'w_o_mem': w_o_mem[l],
            'w_gate': w_gate[l], 'b_gate': b_gate[l], 'w_out': w_out[l],
            'post_mix_g': post_mix_g[l], 'pre_mlp_g': pre_mlp_g[l],
            'w_mlp_up': w_mlp_up[l], 'w_mlp_down': w_mlp_down[l], 'post_mlp_g': post_mlp_g[l],
        }
        mk_p, mv_p = memory_kv(mem_prompt, mem_norm_g[l], w_mem_k[l], w_mem_v[l])
        xp, rows_p = layer_forward(xp, pos_p, None, mk_p, mv_p, p, lam_init)
        past = (cache_mla_ckv[l], cache_mla_krope[l], cache_diff_k[l], cache_diff_v[l])
        xs, rows_s = layer_forward(xs, pos_s, past, cache_mem_k[l], cache_mem_v[l], p, lam_init)
        p_ckv.append(rows_p[0]); p_kr.append(rows_p[1]); p_dk.append(rows_p[2]); p_dv.append(rows_p[3])
        p_mk.append(mk_p); p_mv.append(mv_p)
        s_ckv.append(rows_s[0]); s_kr.append(rows_s[1]); s_dk.append(rows_s[2]); s_dv.append(rows_s[3])
    new_p_ckv = jnp.stack(p_ckv, axis=0)
    new_p_krope = jnp.stack(p_kr, axis=0)
    new_p_dk = jnp.stack(p_dk, axis=0)
    new_p_dv = jnp.stack(p_dv, axis=0)
    new_p_mem_k = jnp.stack(p_mk, axis=0)
    new_p_mem_v = jnp.stack(p_mv, axis=0)
    new_s_ckv = jnp.stack(s_ckv, axis=0)
    new_s_krope = jnp.stack(s_kr, axis=0)
    new_s_dk = jnp.stack(s_dk, axis=0)
    new_s_dv = jnp.stack(s_dv, axis=0)
    return (xp, xs, new_p_ckv, new_p_krope, new_p_dk, new_p_dv, new_p_mem_k, new_p_mem_v,
            new_s_ckv, new_s_krope, new_s_dk, new_s_dv)
```

```python
import functools
import math

import numpy as np
import jax
import jax.numpy as jnp
from jax import lax
from jax.experimental import pallas as pl
from jax.experimental.pallas import tpu as pltpu

F32 = jnp.float32
BF16 = jnp.bfloat16

D_MODEL = 1024
CHUNK = 64
EPS = 1e-6
NEG_INF = -1e30
MLA_HEADS = 8
MLA_Q_RANK = 384
MLA_KV_RANK = 256
MLA_NOPE = 64
MLA_ROPE = 32
MLA_V = 64
MLA_THETA = 10000.0
MLA_SCALE = (MLA_NOPE + MLA_ROPE) ** -0.5
DIFF_HEADS = 8
DIFF_DC = 32
DIFF_V = 64
DIFF_ROT = 8
ROPE_THETA = 500000.0
DIFF_SCALE = DIFF_DC ** -0.5
MEM_HEADS = 4
MEM_DH = 128
MEM_SCALE = MEM_DH ** -0.5
HEAD_PAD = 128
V_ROWS = 80
TQ = 256
TK = 256
VMEM_LIMIT = 56 * 1024 * 1024

_NT = (((1,), (1,)), ((), ()))
_TN = (((0,), (0,)), ((), ()))


def _dot(a, b):
    return jnp.dot(a, b, preferred_element_type=F32)


def _dot_nt(a, b):
    return lax.dot_general(a, b, _NT, preferred_element_type=F32)


def _dot_tn(a, b):
    return lax.dot_general(a, b, _TN, preferred_element_type=F32)


def _rms(x, g):
    return x * lax.rsqrt(jnp.mean(x * x, axis=-1, keepdims=True) + EPS) * g


def _full(shape):
    nd = len(shape)
    return pl.BlockSpec(shape, lambda *_: (0,) * nd)


def _memkv_kernel(mem_ref, g_ref, wk_ref, wv_ref, wvt_ref, k_ref, v_ref, kb_ref, vt_ref):
    mn = _rms(mem_ref[...], g_ref[...]).astype(BF16)
    k = _dot(mn, wk_ref[...])
    k_ref[...] = k
    kb_ref[...] = k.astype(BF16)
    v_ref[...] = _dot(mn, wv_ref[...])
    vt_ref[...] = _dot_nt(wvt_ref[...], mn).astype(BF16)


def _memkv(mem, g, wk, wv, wvt):
    n_mem = mem.shape[0]
    w = wk.shape[1]
    return pl.pallas_call(
        _memkv_kernel,
        out_shape=(jax.ShapeDtypeStruct((n_mem, w), F32), jax.ShapeDtypeStruct((n_mem, w), F32),
                   jax.ShapeDtypeStruct((n_mem, w), BF16), jax.ShapeDtypeStruct((w, n_mem), BF16)),
        name="memkv",
    )(mem, g, wk, wv, wvt)


_N_CQ, _N_CKV, _N_DK, _N_DKP, _N_DV, _N_KR, _N_KRP = 0, 384, 640, 1152, 1664, 2176, 2304
_N_COMMON = 2432
_N_KR128, _N_KR128P = 2432, 2560
_N_PROMPT = 2688
_N_DQ, _N_DQP, _N_MQ = 2432, 2944, 3456
_N_SAMPLE = 3968


def _pre_common(x_ref, g_ref, wn_ref, gq_ref, gkv_ref, c32_ref, s32_ref, cd_ref, sd_ref,
                ckv_o, kr_o, dk_o, dv_o):
    xn = _rms(x_ref[...], g_ref[...]).astype(BF16)
    proj = _dot(xn, wn_ref[...])
    cqn = _rms(proj[:, _N_CQ:_N_CQ + MLA_Q_RANK], gq_ref[...]).astype(BF16)
    ckvn = _rms(proj[:, _N_CKV:_N_CKV + MLA_KV_RANK], gkv_ref[...])
    ckv_o[...] = ckvn
    kr_o[...] = (proj[:, _N_KR:_N_KR + MLA_ROPE] * c32_ref[...]
                 + proj[:, _N_KRP:_N_KRP + MLA_ROPE] * s32_ref[...])
    cd = cd_ref[...]
    sd = sd_ref[...]
    dk_tiles = []
    for j in range(4):
        lo = j * 128
        dk_j = (proj[:, _N_DK + lo:_N_DK + lo + 128] * cd
                + proj[:, _N_DKP + lo:_N_DKP + lo + 128] * sd)
        dk_o[:, lo:lo + 128] = dk_j
        dk_tiles.append(dk_j)
    dv_o[...] = proj[:, _N_DV:_N_DV + 512]
    return xn, proj, cqn, ckvn.astype(BF16), dk_tiles


def _pre_prompt_kernel(x_ref, g_ref, wn_ref, gq_ref, gkv_ref, c32_ref, s32_ref, cd_ref, sd_ref,
                       cm_ref, sm_ref, cmt_ref, smt_ref, cdt_ref, sdt_ref,
                       wt_ref, wuqt_ref, wuqpt_ref, wuk_ref, wuvt_ref,
                       ckv_o, kr_o, dk_o, dv_o, kd_o, km_o, vmt_o, qmt_o, qdt_o, qct_o, vdt_o):
    xn, proj, cqn, ckvb, dk_tiles = _pre_common(
        x_ref, g_ref, wn_ref, gq_ref, gkv_ref, c32_ref, s32_ref, cd_ref, sd_ref,
        ckv_o, kr_o, dk_o, dv_o)
    tm = x_ref.shape[0]
    for j in range(4):
        kd_o[:, j * 128:(j + 1) * 128] = dk_tiles[j].astype(BF16)
    kr128 = (proj[:, _N_KR128:_N_KR128 + 128] * cm_ref[...]
             + proj[:, _N_KR128P:_N_KR128P + 128] * sm_ref[...])
    knope = _dot(ckvb, wuk_ref[...])
    for h in range(MLA_HEADS):
        lo = h * HEAD_PAD
        km_o[:, lo:lo + HEAD_PAD] = (knope[:, lo:lo + HEAD_PAD] + kr128).astype(BF16)
    row = lax.broadcasted_iota(jnp.int32, (V_ROWS - MLA_V, tm), 0)
    ones_rows = jnp.where(row == 0, 1.0, 0.0).astype(BF16)
    vmt = _dot_nt(wuvt_ref[...], ckvb)
    projt = _dot_nt(wt_ref[...], xn)
    for h in range(MLA_HEADS):
        vmt_o[h, 0, 0:MLA_V, :] = vmt[h * MLA_V:(h + 1) * MLA_V, :].astype(BF16)
        vmt_o[h, 0, MLA_V:V_ROWS, :] = ones_rows
        vdt_o[h, 0, 0:DIFF_V, :] = projt[1536 + h * DIFF_V:1536 + (h + 1) * DIFF_V, :].astype(BF16)
        vdt_o[h, 0, DIFF_V:V_ROWS, :] = ones_rows
    qmt = _dot_nt(wuqt_ref[...], cqn)
    qmpt = _dot_nt(wuqpt_ref[...], cqn)
    cmt = cmt_ref[...]
    smt = smt_ref[...]
    for h in range(MLA_HEADS):
        lo = h * HEAD_PAD
        qmt_o[lo:lo + HEAD_PAD, :] = (qmt[lo:lo + HEAD_PAD, :] * cmt
                                      + qmpt[lo:lo + HEAD_PAD, :] * smt).astype(BF16)
    cdt = cdt_ref[...]
    sdt = sdt_ref[...]
    for j in range(4):
        lo = j * 128
        qdt_o[lo:lo + 128, :] = (projt[lo:lo + 128, :] * cdt
                                 + projt[512 + lo:512 + lo + 128, :] * sdt).astype(BF16)
    qct_o[...] = (projt[1024:1536, :] * MEM_SCALE).astype(BF16)


def _pre_sample_kernel(x_ref, g_ref, wn_ref, gq_ref, gkv_ref, c32_ref, s32_ref, cd_ref, sd_ref,
                       cm_ref, sm_ref, wuq_ref, wuqp_ref,
                       ckv_o, kr_o, dk_o, dv_o, qm_o, dq_o, mq_o):
    xn, proj, cqn, ckvb, dk_tiles = _pre_common(
        x_ref, g_ref, wn_ref, gq_ref, gkv_ref, c32_ref, s32_ref, cd_ref, sd_ref,
        ckv_o, kr_o, dk_o, dv_o)
    qm = _dot(cqn, wuq_ref[...])
    qmp = _dot(cqn, wuqp_ref[...])
    cm = cm_ref[...] * MLA_SCALE
    sm = sm_ref[...] * MLA_SCALE
    for h in range(MLA_HEADS):
        lo = h * HEAD_PAD
        qm_o[:, lo:lo + HEAD_PAD] = qm[:, lo:lo + HEAD_PAD] * cm + qmp[:, lo:lo + HEAD_PAD] * sm
    cd = cd_ref[...] * DIFF_SCALE
    sd = sd_ref[...] * DIFF_SCALE
    for j in range(4):
        lo = j * 128
        dq_o[:, lo:lo + 128] = (proj[:, _N_DQ + lo:_N_DQ + lo + 128] * cd
                                + proj[:, _N_DQP + lo:_N_DQP + lo + 128] * sd)
    mq_o[...] = proj[:, _N_MQ:_N_MQ + 512] * MEM_SCALE


def _row_spec(tm, w):
    return pl.BlockSpec((tm, w), lambda i: (i, 0))


def _col_spec(h, tm):
    return pl.BlockSpec((h, tm), lambda i: (0, i))


def _pre_prompt(x, p, tabs):
    rows = x.shape[0]
    tm = TK
    nkb = rows // tm
    in_specs = [
        _row_spec(tm, D_MODEL), _full((1, D_MODEL)), _full(p['wn_prompt'].shape),
        _full((1, MLA_Q_RANK)), _full((1, MLA_KV_RANK)),
        _row_spec(tm, 32), _row_spec(tm, 32), _row_spec(tm, 128), _row_spec(tm, 128),
        _row_spec(tm, 128), _row_spec(tm, 128),
        _col_spec(128, tm), _col_spec(128, tm), _col_spec(128, tm), _col_spec(128, tm),
        _full(p['wt'].shape), _full(p['wuqt'].shape), _full(p['wuqpt'].shape),
        _full(p['wuk_pad'].shape), _full(p['wuvt'].shape),
    ]
    vt_spec = pl.BlockSpec((8, 1, V_ROWS, tm), lambda i: (0, i, 0, 0))
    out_specs = [
        _row_spec(tm, 256), _row_spec(tm, 32), _row_spec(tm, 512), _row_spec(tm, 512),
        _row_spec(tm, 512), _row_spec(tm, 1024), vt_spec,
        _col_spec(1024, tm), _col_spec(512, tm), _col_spec(512, tm), vt_spec,
    ]
    out_shape = [
        jax.ShapeDtypeStruct((rows, 256), F32), jax.ShapeDtypeStruct((rows, 32), F32),
        jax.ShapeDtypeStruct((rows, 512), F32), jax.ShapeDtypeStruct((rows, 512), F32),
        jax.ShapeDtypeStruct((rows, 512), BF16), jax.ShapeDtypeStruct((rows, 1024), BF16),
        jax.ShapeDtypeStruct((8, nkb, V_ROWS, tm), BF16),
        jax.ShapeDtypeStruct((1024, rows), BF16), jax.ShapeDtypeStruct((512, rows), BF16),
        jax.ShapeDtypeStruct((512, rows), BF16),
        jax.ShapeDtypeStruct((8, nkb, V_ROWS, tm), BF16),
    ]
    return pl.pallas_call(
        _pre_prompt_kernel,
        grid=(rows // tm,),
        in_specs=in_specs, out_specs=out_specs, out_shape=out_shape,
        compiler_params=pltpu.CompilerParams(dimension_semantics=("parallel",),
                                             vmem_limit_bytes=VMEM_LIMIT),
        name="pre_prompt",
    )(x, p['pre_mix_g'], p['wn_prompt'], p['gq'], p['gkv'],
      tabs['c32'], tabs['s32'], tabs['cd'], tabs['sd'], tabs['cm'], tabs['sm'],
      tabs['cmt'], tabs['smt'], tabs['cdt'], tabs['sdt'],
      p['wt'], p['wuqt'], p['wuqpt'], p['wuk_pad'], p['wuvt'])


def _pre_sample(x, p, tabs):
    rows = x.shape[0]
    tm = 256
    in_specs = [
        _row_spec(tm, D_MODEL), _full((1, D_MODEL)), _full(p['wn_sample'].shape),
        _full((1, MLA_Q_RANK)), _full((1, MLA_KV_RANK)),
        _row_spec(tm, 32), _row_spec(tm, 32), _row_spec(tm, 128), _row_spec(tm, 128),
        _row_spec(tm, 128), _row_spec(tm, 128),
        _full(p['wuq_pad'].shape), _full(p['wuqp_pad'].shape),
    ]
    out_specs = [
        _row_spec(tm, 256), _row_spec(tm, 32), _row_spec(tm, 512), _row_spec(tm, 512),
        _row_spec(tm, 1024), _row_spec(tm, 512), _row_spec(tm, 512),
    ]
    out_shape = [
        jax.ShapeDtypeStruct((rows, 256), F32), jax.ShapeDtypeStruct((rows, 32), F32),
        jax.ShapeDtypeStruct((rows, 512), F32), jax.ShapeDtypeStruct((rows, 512), F32),
        jax.ShapeDtypeStruct((rows, 1024), F32), jax.ShapeDtypeStruct((rows, 512), F32),
        jax.ShapeDtypeStruct((rows, 512), F32),
    ]
    return pl.pallas_call(
        _pre_sample_kernel,
        grid=(rows // tm,),
        in_specs=in_specs, out_specs=out_specs, out_shape=out_shape,
        compiler_params=pltpu.CompilerParams(dimension_semantics=("parallel",),
                                             vmem_limit_bytes=VMEM_LIMIT),
        name="pre_sample",
    )(x, p['pre_mix_g'], p['wn_sample'], p['gq'], p['gkv'],
      tabs['c32'], tabs['s32'], tabs['cd'], tabs['sd'], tabs['cm'], tabs['sm'],
      p['wuq_pad'], p['wuqp_pad'])


def _chunk_mask(tk, tq):
    kc = lax.broadcasted_iota(jnp.int32, (tk, tq), 0) // CHUNK
    qc = lax.broadcasted_iota(jnp.int32, (tk, tq), 1) // CHUNK
    return kc <= qc


def _online_step(st, m, acc, vt):
    m_new = jnp.maximum(m, jnp.max(st, axis=0, keepdims=True))
    alpha = jnp.exp(m - m_new)
    p = jnp.exp(st - m_new).astype(BF16)
    return m_new, alpha * acc + _dot(vt, p)


def _attn_mla_kernel(qt_ref, k_ref, vt_ref, o_ref):
    i = pl.program_id(1)
    heads = qt_ref.shape[0] // HEAD_PAD
    qs = [qt_ref[h * HEAD_PAD:(h + 1) * HEAD_PAD, :] for h in range(heads)]

    def scores(j, h):
        k = k_ref[pl.ds(pl.multiple_of(j * TK, TK), TK), h * HEAD_PAD:(h + 1) * HEAD_PAD]
        return _dot(k, qs[h])

    def body(j, carry):
        out = []
        for h in range(heads):
            m, acc = carry[h]
            out.append(_online_step(scores(j, h), m, acc, vt_ref[h, j]))
        return tuple(out)

    init = tuple((jnp.full((1, TQ), NEG_INF, F32), jnp.zeros((V_ROWS, TQ), F32))
                 for _ in range(heads))
    carry = lax.fori_loop(0, i, body, init)
    mask = _chunk_mask(TK, TQ)
    for h in range(heads):
        m, acc = carry[h]
        st = jnp.where(mask, scores(i, h), NEG_INF)
        m, acc = _online_step(st, m, acc, vt_ref[h, i])
        o_ref[h * MLA_V:(h + 1) * MLA_V, :] = (
            acc[0:MLA_V, :] / acc[MLA_V:MLA_V + 1, :]).astype(BF16)


def _attn_mla(qmt, km, vmt):
    t = km.shape[0]
    hg = 2
    groups = MLA_HEADS // hg
    nq = t // TQ
    return pl.pallas_call(
        _attn_mla_kernel,
        grid=(groups, nq),
        in_specs=[
            pl.BlockSpec((hg * HEAD_PAD, TQ), lambda g, i: (g, i)),
            pl.BlockSpec((t, hg * HEAD_PAD), lambda g, i: (0, g)),
            pl.BlockSpec((hg, t // TK, V_ROWS, TK), lambda g, i: (g, 0, 0, 0)),
        ],
        out_specs=pl.BlockSpec((hg * MLA_V, TQ), lambda g, i: (g, i)),
        out_shape=jax.ShapeDtypeStruct((MLA_HEADS * MLA_V, t), BF16),
        compiler_params=pltpu.CompilerParams(dimension_semantics=("parallel", "parallel"),
                                             vmem_limit_bytes=VMEM_LIMIT),
        name="attn_mla",
    )(qmt, km, vmt)


def _attn_diff_kernel(lam_init, qt_ref, k_ref, vt_ref, lq1_ref, lk1_ref, lq2_ref, lk2_ref,
                      gsub_ref, o_ref, acc_ref, m_ref):
    i = pl.program_id(1)
    qt = qt_ref[...]
    grp = lax.broadcasted_iota(jnp.int32, (128, TQ), 0) // DIFF_DC
    zero = jnp.zeros_like(qt)
    wq = jnp.concatenate([jnp.where(grp == s, qt, zero) for s in range(4)], axis=1)

    def step(j, masked):
        k = k_ref[pl.ds(pl.multiple_of(j * TK, TK), TK), :]
        st = _dot(k, wq)
        if masked:
            mask = _chunk_mask(TK, TQ)
            st = jnp.where(jnp.concatenate([mask] * 4, axis=1), st, NEG_INF)
        m = m_ref[...]
        m_new = jnp.maximum(m, jnp.max(st, axis=0, keepdims=True))
        alpha = jnp.exp(m - m_new)
        m_ref[...] = m_new
        p = jnp.exp(st - m_new).astype(BF16)
        for h in range(2):
            sl = slice(h * 2 * TQ, (h + 1) * 2 * TQ)
            acc_ref[h] = alpha[:, sl] * acc_ref[h] + _dot(vt_ref[h, j], p[:, sl])

    m_ref[...] = jnp.full(m_ref.shape, NEG_INF, F32)
    acc_ref[...] = jnp.zeros(acc_ref.shape, F32)

    def body(j, c):
        step(j, False)
        return c

    lax.fori_loop(0, i, body, 0)
    step(i, True)

    lam = (jnp.exp(jnp.sum(lq1_ref[...] * lk1_ref[...], axis=-1, keepdims=True))
           - jnp.exp(jnp.sum(lq2_ref[...] * lk2_ref[...], axis=-1, keepdims=True))
           + lam_init)
    g = gsub_ref[...] * (1.0 - lam_init)
    for h in range(2):
        acc = acc_ref[h]
        o0 = acc[0:DIFF_V, 0:TQ] / acc[DIFF_V:DIFF_V + 1, 0:TQ]
        o1 = acc[0:DIFF_V, TQ:2 * TQ] / acc[DIFF_V:DIFF_V + 1, TQ:2 * TQ]
        o = o0 - lam * o1
        o = o * lax.rsqrt(jnp.mean(o * o, axis=0, keepdims=True) + EPS) * g
        o_ref[h * DIFF_V:(h + 1) * DIFF_V, :] = o.astype(BF16)


def _attn_diff(qdt, kd, vdt, lq1, lk1, lq2, lk2, gsub, lam_init):
    t = kd.shape[0]
    nq = t // TQ
    pairs = DIFF_HEADS // 2
    vec = _full((1, DIFF_DC))
    return pl.pallas_call(
        functools.partial(_attn_diff_kernel, lam_init),
        grid=(pairs, nq),
        in_specs=[
            pl.BlockSpec((128, TQ), lambda g, i: (g, i)),
            pl.BlockSpec((t, 128), lambda g, i: (0, g)),
            pl.BlockSpec((2, t // TK, V_ROWS, TK), lambda g, i: (g, 0, 0, 0)),
            vec, vec, vec, vec, _full((DIFF_V, 1)),
        ],
        out_specs=pl.BlockSpec((2 * DIFF_V, TQ), lambda g, i: (g, i)),
        out_shape=jax.ShapeDtypeStruct((DIFF_HEADS * DIFF_V, t), BF16),
        scratch_shapes=[pltpu.VMEM((2, V_ROWS, 2 * TQ), F32), pltpu.VMEM((1, 4 * TQ), F32)],
        compiler_params=pltpu.CompilerParams(dimension_semantics=("parallel", "parallel"),
                                             vmem_limit_bytes=VMEM_LIMIT),
        name="attn_diff",
    )(qdt, kd, vdt, lq1, lk1, lq2, lk2, gsub)


def _attn_mem_kernel(qt_ref, k_ref, vt_ref, o_ref):
    for h in range(MEM_HEADS):
        sl = slice(h * MEM_DH, (h + 1) * MEM_DH)
        st = _dot(k_ref[:, sl], qt_ref[sl, :])
        e = jnp.exp(st - jnp.max(st, axis=0, keepdims=True))
        p = (e / jnp.sum(e, axis=0, keepdims=True)).astype(BF16)
        o_ref[sl, :] = _dot(vt_ref[sl, :], p).astype(BF16)


def _attn_mem(qct, mk, mvt):
    w, t = qct.shape
    tq = 512
    return pl.pallas_call(
        _attn_mem_kernel,
        grid=(t // tq,),
        in_specs=[pl.BlockSpec((w, tq), lambda i: (0, i)), _full(mk.shape), _full(mvt.shape)],
        out_specs=pl.BlockSpec((w, tq), lambda i: (0, i)),
        out_shape=jax.ShapeDtypeStruct((w, t), BF16),
        compiler_params=pltpu.CompilerParams(dimension_semantics=("parallel",)),
        name="attn_mem",
    )(qct, mk, mvt)


def _softmax_two(s_a, s_b):
    m = jnp.maximum(jnp.max(s_a, axis=-1, keepdims=True), jnp.max(s_b, axis=-1, keepdims=True))
    e_a = jnp.exp(s_a - m)
    e_b = jnp.exp(s_b - m)
    inv = 1.0 / (jnp.sum(e_a, axis=-1, keepdims=True) + jnp.sum(e_b, axis=-1, keepdims=True))
    return e_a * inv, e_b * inv


def _diag_blocks(o_all, rows, width, nblk):
    lane_blk = lax.broadcasted_iota(jnp.int32, (rows, nblk * width), 1) // width
    out = jnp.zeros((rows, nblk * width), F32)
    for b in range(nblk):
        out = out + jnp.where(lane_blk == b, o_all[b * rows:(b + 1) * rows, :], 0.0)
    return out


def _sample_mla_kernel(qm_ref, ckvn_ref, krn_ref, mq_ref, ckvp_ref, krp_ref, mk_ref, mv_ref,
                       wabs_ref, wuv_ref, oa_ref, oc_ref):
    nq = qm_ref.shape[0]
    qm = qm_ref[...].astype(BF16)
    qext = jnp.concatenate(
        [_dot(qm[:, h * HEAD_PAD:(h + 1) * HEAD_PAD], wabs_ref[h]) for h in range(MLA_HEADS)],
        axis=0).astype(BF16)
    q_lat = qext[:, 0:MLA_KV_RANK]
    q_rope = qext[:, MLA_KV_RANK:MLA_KV_RANK + MLA_ROPE]
    ckv_p = ckvp_ref[...].astype(BF16)
    ckv_n = ckvn_ref[...].astype(BF16)
    s_p = _dot_nt(q_lat, ckv_p) + _dot_nt(q_rope, krp_ref[...].astype(BF16))
    s_n = _dot_nt(q_lat, ckv_n) + _dot_nt(q_rope, krn_ref[...].astype(BF16))
    p_p, p_n = _softmax_two(s_p, s_n)
    o_lat = _dot(p_p.astype(BF16), ckv_p) + _dot(p_n.astype(BF16), ckv_n)
    o_all = _dot(o_lat.astype(BF16), wuv_ref[...])
    oa_ref[...] = _diag_blocks(o_all, nq, MLA_V, MLA_HEADS)
    mq = mq_ref[...].astype(BF16)
    for h in range(MEM_HEADS):
        sl = slice(h * MEM_DH, (h + 1) * MEM_DH)
        s = _dot_nt(mq[:, sl], mk_ref[:, sl].astype(BF16))
        e = jnp.exp(s - jnp.max(s, axis=-1, keepdims=True))
        p = (e / jnp.sum(e, axis=-1, keepdims=True)).astype(BF16)
        oc_ref[:, sl] = _dot(p, mv_ref[:, sl].astype(BF16))


def _sample_mla(qm, ckv_new, kr_new, mq, ckv_past, kr_past, mem_k, mem_v, wabs, wuv, nb, nq):
    def rows(w):
        return pl.BlockSpec((nq, w), lambda b: (b, 0))

    def cache(shape):
        return pl.BlockSpec((None,) + shape, lambda b: (b, 0, 0))

    past = ckv_past.shape[1]
    return pl.pallas_call(
        _sample_mla_kernel,
        grid=(nb,),
        in_specs=[rows(1024), rows(256), rows(32), rows(512),
                  cache((past, 256)), cache((past, 32)), cache(mem_k.shape[1:]),
                  cache(mem_v.shape[1:]), _full(wabs.shape), _full(wuv.shape)],
        out_specs=[rows(512), rows(512)],
        out_shape=[jax.ShapeDtypeStruct((nb * nq, 512), F32)] * 2,
        compiler_params=pltpu.CompilerParams(dimension_semantics=("parallel",),
                                             vmem_limit_bytes=VMEM_LIMIT),
        name="sample_mla_mem",
    )(qm, ckv_new, kr_new, mq, ckv_past, kr_past, mem_k, mem_v, wabs, wuv)


def _sample_diff_kernel(lam_init, dq_ref, dkn_ref, dvn_ref, dkp_ref, dvp_ref,
                        lq1_ref, lk1_ref, lq2_ref, lk2_ref, o_ref):
    nq = dq_ref.shape[0]
    w = dq_ref.shape[1]
    nmaps = w // DIFF_DC
    q = dq_ref[...]
    row_grp = lax.broadcasted_iota(jnp.int32, (nmaps * nq, w), 0) // nq
    lane_grp = lax.broadcasted_iota(jnp.int32, (nmaps * nq, w), 1) // DIFF_DC
    qbd = jnp.where(row_grp == lane_grp, jnp.concatenate([q] * nmaps, axis=0), 0.0).astype(BF16)
    k_p = dkp_ref[...].astype(BF16)
    k_n = dkn_ref[...].astype(BF16)
    p_p, p_n = _softmax_two(_dot_nt(qbd, k_p), _dot_nt(qbd, k_n))
    lam = (jnp.exp(jnp.sum(lq1_ref[...] * lk1_ref[...], axis=-1, keepdims=True))
           - jnp.exp(jnp.sum(lq2_ref[...] * lk2_ref[...], axis=-1, keepdims=True))
           + lam_init)

    def combine(p):
        parts = [p[(2 * h) * nq:(2 * h + 1) * nq, :] - lam * p[(2 * h + 1) * nq:(2 * h + 2) * nq, :]
                 for h in range(nmaps // 2)]
        return jnp.concatenate(parts, axis=0).astype(BF16)

    o_all = (_dot(combine(p_p), dvp_ref[...].astype(BF16))
             + _dot(combine(p_n), dvn_ref[...].astype(BF16)))
    o_ref[...] = _diag_blocks(o_all, nq, DIFF_V, nmaps // 2)


def _sample_diff(dq, dk_new, dv_new, dk_past, dv_past, lq1, lk1, lq2, lk2, lam_init, nb, nq):
    hw = 256
    past = dk_past.shape[1]
    rows = pl.BlockSpec((nq, hw), lambda b, g: (b, g))
    cache = pl.BlockSpec((None, past, hw), lambda b, g: (b, 0, g))
    vec = _full((1, DIFF_DC))
    return pl.pallas_call(
        functools.partial(_sample_diff_kernel, lam_init),
        grid=(nb, 512 // hw),
        in_specs=[rows, rows, rows, cache, cache, vec, vec, vec, vec],
        out_specs=rows,
        out_shape=jax.ShapeDtypeStruct((nb * nq, 512), F32),
        compiler_params=pltpu.CompilerParams(dimension_semantics=("parallel", "parallel"),
                                             vmem_limit_bytes=VMEM_LIMIT),
        name="sample_diff",
    )(dq, dk_new, dv_new, dk_past, dv_past, lq1, lk1, lq2, lk2)


def _subln_kernel(lam_init, o_ref, g_ref, out_ref):
    g = g_ref[...] * (1.0 - lam_init)
    for h in range(DIFF_HEADS):
        o = o_ref[h * DIFF_V:(h + 1) * DIFF_V, :]
        out_ref[h * DIFF_V:(h + 1) * DIFF_V, :] = (
            o * lax.rsqrt(jnp.mean(o * o, axis=0, keepdims=True) + EPS) * g).astype(BF16)


def _subln(ot, gsub, lam_init):
    return pl.pallas_call(
        functools.partial(_subln_kernel, lam_init),
        out_shape=jax.ShapeDtypeStruct(ot.shape, BF16),
        name="sample_subln",
    )(ot, gsub)


def _mix_kernel(x_ref, oa_ref, ob_ref, oc_ref, g_ref, wg_ref, bg_ref, woa_ref, wob_ref, woc_ref,
                wout_ref, gpost_ref, y_ref):
    x = x_ref[...]
    xn = _rms(x, g_ref[...]).astype(BF16)
    d = x.shape[1]
    merged = None
    for b, (o_ref, w_ref) in enumerate(((oa_ref, woa_ref), (ob_ref, wob_ref), (oc_ref, woc_ref))):
        gate = jax.nn.sigmoid(_dot(xn, wg_ref[:, b * d:(b + 1) * d]) + bg_ref[:, b * d:(b + 1) * d])
        term = gate * _dot_tn(o_ref[...], w_ref[...])
        merged = term if merged is None else merged + term
    mix = _dot(merged.astype(BF16), wout_ref[...])
    y_ref[...] = x + _rms(mix, gpost_ref[...])


def _mix(x, oat, obt, oct, p):
    rows, d = x.shape
    tm = 512 if rows % 512 == 0 else rows
    ot_spec = pl.BlockSpec((512, tm), lambda i: (0, i))
    return pl.pallas_call(
        _mix_kernel,
        grid=(rows // tm,),
        in_specs=[_row_spec(tm, d), ot_spec, ot_spec, ot_spec, _full((1, d)),
                  _full(p['w_gate'].shape), _full((1, 3 * d)), _full((512, d)), _full((512, d)),
                  _full((512, d)), _full((d, d)), _full((1, d))],
        out_specs=_row_spec(tm, d),
        out_shape=jax.ShapeDtypeStruct((rows, d), F32),
        compiler_params=pltpu.CompilerParams(dimension_semantics=("parallel",),
                                             vmem_limit_bytes=VMEM_LIMIT),
        name="mix",
    )(x, oat, obt, oct, p['pre_mix_g'], p['w_gate'], p['b_gate'], p['w_o_mla'], p['w_o_diff'],
      p['w_o_mem'], p['w_out'], p['post_mix_g'])


def _mlp_kernel(x_ref, g_ref, wup_ref, wdn_ref, gpost_ref, y_ref):
    x = x_ref[...]
    h = _rms(x, g_ref[...]).astype(BF16)
    u = jnp.maximum(_dot(h, wup_ref[...]), 0.0)
    f = _dot((u * u).astype(BF16), wdn_ref[...])
    y_ref[...] = x + _rms(f, gpost_ref[...])


def _mlp(x, p):
    rows, d = x.shape
    tm = 256
    return pl.pallas_call(
        _mlp_kernel,
        grid=(rows // tm,),
        in_specs=[_row_spec(tm, d), _full((1, d)), _full(p['w_mlp_up'].shape),
                  _full(p['w_mlp_down'].shape), _full((1, d))],
        out_specs=_row_spec(tm, d),
        out_shape=jax.ShapeDtypeStruct((rows, d), F32),
        compiler_params=pltpu.CompilerParams(dimension_semantics=("parallel",),
                                             vmem_limit_bytes=VMEM_LIMIT),
        name="mlp",
    )(x, p['pre_mlp_g'], p['w_mlp_up'], p['w_mlp_down'], p['post_mlp_g'])


def _partner(width, group, half):
    idx = np.arange(width)
    sign = np.zeros(width, np.float32)
    d = idx % group
    first = d < half
    second = (d >= half) & (d < 2 * half)
    src = np.where(first, idx + half, np.where(second, idx - half, idx))
    sign[first] = -1.0
    sign[second] = 1.0
    return src, sign


def _take_signed(w, src, sign):
    return w[:, src] * jnp.asarray(sign)[None, :]


def _pad_cols(w, total):
    return jnp.pad(w, ((0, 0), (0, total - w.shape[1])))


def _prep_layer(l, w_in, mla_w_uq, mla_w_uk, mla_w_uv, w_mem_k, w_mem_v, w_o_mla, w_o_diff, w_o_mem,
                w_gate, b_gate, w_out, w_mlp_up, w_mlp_down, gains):
    cq, ckv, kr, dq, dk, dv, mq = jnp.split(
        w_in[l], np.cumsum((384, 256, 32, 512, 512, 512))[...].tolist(), axis=1)
    src_d, sign_d = _partner(512, DIFF_DC, DIFF_ROT // 2)
    src_r, sign_r = _partner(32, 32, MLA_ROPE // 2)
    dkp = _take_signed(dk, src_d, sign_d)
    dqp = _take_signed(dq, src_d, sign_d)
    krp = _take_signed(kr, src_r, sign_r)
    common = [cq, ckv, dk, dkp, dv, _pad_cols(kr, 128), _pad_cols(krp, 128)]
    zeros64 = jnp.zeros((D_MODEL, MLA_NOPE), F32)
    kr128 = _pad_cols(jnp.concatenate([zeros64, kr], axis=1), 128)
    kr128p = _pad_cols(jnp.concatenate([zeros64, krp], axis=1), 128)
    head_w = MLA_NOPE + MLA_ROPE
    e = np.arange(HEAD_PAD)
    valid = e < head_w
    src_q = np.concatenate([h * head_w + np.where(valid, e, 0) for h in range(MLA_HEADS)])
    sign_q = np.tile(valid.astype(np.float32), MLA_HEADS)
    in_rope1 = (e >= MLA_NOPE) & (e < MLA_NOPE + MLA_ROPE // 2)
    in_rope2 = (e >= MLA_NOPE + MLA_ROPE // 2) & valid
    pe = np.where(in_rope1, e + MLA_ROPE // 2, np.where(in_rope2, e - MLA_ROPE // 2, 0))
    src_qp = np.concatenate([h * head_w + pe for h in range(MLA_HEADS)])
    sign_qp = np.tile(np.where(in_rope1, -1.0, np.where(in_rope2, 1.0, 0.0)).astype(np.float32),
                      MLA_HEADS)
    wuq_pad = _take_signed(mla_w_uq[l], src_q, sign_q)
    wuqp_pad = _take_signed(mla_w_uq[l], src_qp, sign_qp)
    wuk = mla_w_uk[l]
    wuk_pad = jnp.pad(wuk, ((0, 0), (0, 0), (0, HEAD_PAD - MLA_NOPE))).reshape(MLA_KV_RANK, -1)
    wuv = mla_w_uv[l].reshape(MLA_KV_RANK, MLA_HEADS * MLA_V)
    sel = np.zeros((HEAD_PAD, 128), np.float32)
    sel[MLA_NOPE + np.arange(MLA_ROPE), np.arange(MLA_ROPE)] = 1.0
    wabs = jnp.concatenate([
        jnp.pad(jnp.transpose(wuk, (1, 2, 0)), ((0, 0), (0, HEAD_PAD - MLA_NOPE), (0, 0))),
        jnp.broadcast_to(jnp.asarray(sel), (MLA_HEADS, HEAD_PAD, 128))], axis=2)
    bf = lambda a: a.astype(BF16)
    row = lambda a: a[l][None, :]
    p = {
        'wn_prompt': bf(jnp.concatenate(common + [kr128, kr128p], axis=1)),
        'wn_sample': bf(jnp.concatenate(common + [dq, dqp, mq], axis=1)),
        'wt': bf(jnp.concatenate([dq, dqp, mq, dv], axis=1).T),
        'wuqt': bf(wuq_pad.T), 'wuqpt': bf(wuqp_pad.T),
        'wuq_pad': bf(wuq_pad), 'wuqp_pad': bf(wuqp_pad),
        'wuk_pad': bf(wuk_pad), 'wuvt': bf(wuv.T), 'wuv': bf(wuv), 'wabs': bf(wabs),
        'w_mem_k': bf(w_mem_k[l]), 'w_mem_v': bf(w_mem_v[l]), 'w_mem_vt': bf(w_mem_v[l].T),
        'w_o_mla': bf(w_o_mla[l]), 'w_o_diff': bf(w_o_diff[l]), 'w_o_mem': bf(w_o_mem[l]),
        'w_gate': bf(w_gate[l]), 'b_gate': row(b_gate), 'w_out': bf(w_out[l]),
        'w_mlp_up': bf(w_mlp_up[l]), 'w_mlp_down': bf(w_mlp_down[l]),
    }
    for name, g in gains.items():
        p[name] = row(g)
    return p


def _rope_cos_sin(pos, rot_dim, theta):
    half = rot_dim // 2
    inv = jnp.power(jnp.float32(theta), -jnp.arange(half, dtype=F32) * (2.0 / rot_dim))
    ang = pos.astype(F32)[:, None] * inv[None, :]
    return jnp.cos(ang), jnp.sin(ang)


def _tables(pos, reps):
    n = pos.shape[0]
    cm, sm = _rope_cos_sin(pos, MLA_ROPE, MLA_THETA)
    cd, sd = _rope_cos_sin(pos, DIFF_ROT, ROPE_THETA)
    one = lambda w: jnp.ones((n, w), F32)
    zero = lambda w: jnp.zeros((n, w), F32)
    t = {
        'c32': jnp.concatenate([cm, cm], axis=1), 's32': jnp.concatenate([sm, sm], axis=1),
        'cm': jnp.concatenate([one(64), cm, cm, one(32)], axis=1),
        'sm': jnp.concatenate([zero(64), sm, sm, zero(32)], axis=1),
        'cd': jnp.tile(jnp.concatenate([cd, cd, one(24)], axis=1), (1, 4)),
        'sd': jnp.tile(jnp.concatenate([sd, sd, zero(24)], axis=1), (1, 4)),
    }
    return {k: jnp.tile(v, (reps, 1)) for k, v in t.items()}


def kernel(x_prompt, x_sample, cache_mla_ckv, cache_mla_krope, cache_diff_k, cache_diff_v, cache_mem_k, cache_mem_v, mem_prompt, pre_mix_g, w_in, mla_q_norm_g, mla_w_uq, mla_kv_norm_g, mla_w_uk, mla_w_uv, diff_lq1, diff_lk1, diff_lq2, diff_lk2, diff_subln_g, mem_norm_g, w_mem_k, w_mem_v, w_o_mla, w_o_diff, w_o_mem, w_gate, b_gate, w_out, post_mix_g, pre_mlp_g, w_mlp_up, w_mlp_down, post_mlp_g):
    depth = w_in.shape[0]
    bp, t, d = x_prompt.shape
    nb, nq, _ = x_sample.shape
    past = cache_mla_ckv.shape[2]
    assert bp == 1 and t % TQ == 0 and TQ == TK and d == D_MODEL
    assert past % CHUNK == 0 and nq <= CHUNK

    tabs_p = _tables(jnp.arange(t, dtype=jnp.int32), 1)
    tabs_p['cmt'] = (tabs_p['cm'] * MLA_SCALE).T
    tabs_p['smt'] = (tabs_p['sm'] * MLA_SCALE).T
    tabs_p['cdt'] = (tabs_p['cd'] * DIFF_SCALE).T
    tabs_p['sdt'] = (tabs_p['sd'] * DIFF_SCALE).T
    tabs_s = _tables(past + jnp.arange(nq, dtype=jnp.int32), nb)

    xp = x_prompt.reshape(t, d)
    xs = x_sample.reshape(nb * nq, d)
    outs = {k: [] for k in ('p_ckv', 'p_kr', 'p_dk', 'p_dv', 'p_mk', 'p_mv',
                            's_ckv', 's_kr', 's_dk', 's_dv')}
    for l in range(depth):
        lam_init = 0.8 - 0.6 * math.exp(-0.3 * l)
        p = _prep_layer(l, w_in, mla_w_uq, mla_w_uk, mla_w_uv, w_mem_k, w_mem_v, w_o_mla, w_o_diff,
                        w_o_mem, w_gate, b_gate, w_out, w_mlp_up, w_mlp_down,
                        {'pre_mix_g': pre_mix_g, 'gq': mla_q_norm_g, 'gkv': mla_kv_norm_g,
                         'mem_norm_g': mem_norm_g, 'post_mix_g': post_mix_g,
                         'pre_mlp_g': pre_mlp_g, 'post_mlp_g': post_mlp_g})
        lq1, lk1, lq2, lk2 = (a[l][None, :] for a in (diff_lq1, diff_lk1, diff_lq2, diff_lk2))
        gsub = diff_subln_g[l][:, None]

        mk, mv, mkb, mvt = _memkv(mem_prompt[0], p['mem_norm_g'], p['w_mem_k'], p['w_mem_v'],
                                  p['w_mem_vt'])
        (ckv_p, kr_p, dk_p, dv_p, kd, km, vmt, qmt, qdt, qct, vdt) = _pre_prompt(xp, p, tabs_p)
        oat = _attn_mla(qmt, km, vmt)
        obt = _attn_diff(qdt, kd, vdt, lq1, lk1, lq2, lk2, gsub, lam_init)
        oct = _attn_mem(qct, mkb, mvt)
        xp = _mlp(_mix(xp, oat, obt, oct, p), p)

        (ckv_s, kr_s, dk_s, dv_s, qm_s, dq_s, mq_s) = _pre_sample(xs, p, tabs_s)
        oa_s, oc_s = _sample_mla(qm_s, ckv_s, kr_s, mq_s, cache_mla_ckv[l], cache_mla_krope[l],
                                 cache_mem_k[l].reshape(nb, -1, 512),
                                 cache_mem_v[l].reshape(nb, -1, 512), p['wabs'], p['wuv'], nb, nq)
        ob_s = _sample_diff(dq_s, dk_s, dv_s, cache_diff_k[l].reshape(nb, past, 512),
                            cache_diff_v[l].reshape(nb, past, 512), lq1, lk1, lq2, lk2,
                            lam_init, nb, nq)
        obt_s = _subln(ob_s.T, gsub, lam_init)
        xs = _mlp(_mix(xs, oa_s.T.astype(BF16), obt_s, oc_s.T.astype(BF16), p), p)

        outs['p_ckv'].append(ckv_p.reshape(1, t, MLA_KV_RANK))
        outs['p_kr'].append(kr_p.reshape(1, t, MLA_ROPE))
        outs['p_dk'].append(dk_p.reshape(1, t, DIFF_HEADS, DIFF_V))
        outs['p_dv'].append(dv_p.reshape(1, t, DIFF_HEADS, DIFF_V))
        outs['p_mk'].append(mk.reshape(1, -1, MEM_HEADS, MEM_DH))
        outs['p_mv'].append(mv.reshape(1, -1, MEM_HEADS, MEM_DH))
        outs['s_ckv'].append(ckv_s.reshape(nb, nq, MLA_KV_RANK))
        outs['s_kr'].append(kr_s.reshape(nb, nq, MLA_ROPE))
        outs['s_dk'].append(dk_s.reshape(nb, nq, DIFF_HEADS, DIFF_V))
        outs['s_dv'].append(dv_s.reshape(nb, nq, DIFF_HEADS, DIFF_V))

    st = lambda k: jnp.stack(outs[k], axis=0)
    return (xp.reshape(1, t, d), xs.reshape(nb, nq, d),
            st('p_ckv'), st('p_kr'), st('p_dk'), st('p_dv'), st('p_mk'), st('p_mv'),
            st('s_ckv'), st('s_kr'), st('s_dk'), st('s_dv'))
```

```python
import functools
import math

import numpy as np
import jax
import jax.numpy as jnp
from jax import lax
from jax.experimental import pallas as pl
from jax.experimental.pallas import tpu as pltpu

F32 = jnp.float32
BF16 = jnp.bfloat16

D_MODEL = 1024
CHUNK = 64
EPS = 1e-6
NEG_INF = -1e30
MLA_HEADS = 8
MLA_Q_RANK = 384
MLA_KV_RANK = 256
MLA_NOPE = 64
MLA_ROPE = 32
MLA_V = 64
MLA_THETA = 10000.0
MLA_SCALE = (MLA_NOPE + MLA_ROPE) ** -0.5
DIFF_HEADS = 8
DIFF_DC = 32
DIFF_V = 64
DIFF_ROT = 8
ROPE_THETA = 500000.0
DIFF_SCALE = DIFF_DC ** -0.5
MEM_HEADS = 4
MEM_DH = 128
MEM_SCALE = MEM_DH ** -0.5
HEAD_PAD = 128
V_ROWS = 80
TQ = 512
TK = 256
VMEM_LIMIT = 56 * 1024 * 1024

_NT = (((1,), (1,)), ((), ()))
_TN = (((0,), (0,)), ((), ()))


def _dot(a, b):
    return jnp.dot(a, b, preferred_element_type=F32)


def _dot_nt(a, b):
    return lax.dot_general(a, b, _NT, preferred_element_type=F32)


def _dot_tn(a, b):
    return lax.dot_general(a, b, _TN, preferred_element_type=F32)


def _rms(x, g):
    return x * lax.rsqrt(jnp.mean(x * x, axis=-1, keepdims=True) + EPS) * g


def _full(shape):
    nd = len(shape)
    return pl.BlockSpec(shape, lambda *_: (0,) * nd)


def _memkv_kernel(mem_ref, g_ref, wk_ref, wv_ref, wvt_ref, k_ref, v_ref, kb_ref, vt_ref):
    mn = _rms(mem_ref[...], g_ref[...]).astype(BF16)
    k = _dot(mn, wk_ref[...])
    k_ref[...] = k
    kb_ref[...] = k.astype(BF16)
    v_ref[...] = _dot(mn, wv_ref[...])
    vt_ref[...] = _dot_nt(wvt_ref[...], mn).astype(BF16)


def _memkv(mem, g, wk, wv, wvt):
    n_mem = mem.shape[0]
    w = wk.shape[1]
    return pl.pallas_call(
        _memkv_kernel,
        out_shape=(jax.ShapeDtypeStruct((n_mem, w), F32), jax.ShapeDtypeStruct((n_mem, w), F32),
                   jax.ShapeDtypeStruct((n_mem, w), BF16), jax.ShapeDtypeStruct((w, n_mem), BF16)),
        name="memkv",
    )(mem, g, wk, wv, wvt)


_N_CQ, _N_CKV, _N_DK, _N_DKP, _N_DV, _N_KR, _N_KRP = 0, 384, 640, 1152, 1664, 2176, 2304
_N_COMMON = 2432
_N_KR128, _N_KR128P = 2432, 2560
_N_PROMPT = 2688
_N_DQ, _N_DQP, _N_MQ = 2432, 2944, 3456
_N_SAMPLE = 3968


def _pre_common(x_ref, g_ref, wn_ref, gq_ref, gkv_ref, c32_ref, s32_ref, cd_ref, sd_ref,
                ckv_o, kr_o, dk_o, dv_o):
    xn = _rms(x_ref[...], g_ref[...]).astype(BF16)
    proj = _dot(xn, wn_ref[...])
    cqn = _rms(proj[:, _N_CQ:_N_CQ + MLA_Q_RANK], gq_ref[...]).astype(BF16)
    ckvn = _rms(proj[:, _N_CKV:_N_CKV + MLA_KV_RANK], gkv_ref[...])
    ckv_o[...] = ckvn
    kr_o[...] = (proj[:, _N_KR:_N_KR + MLA_ROPE] * c32_ref[...]
                 + proj[:, _N_KRP:_N_KRP + MLA_ROPE] * s32_ref[...])
    cd = cd_ref[...]
    sd = sd_ref[...]
    dk_tiles = []
    for j in range(4):
        lo = j * 128
        dk_j = (proj[:, _N_DK + lo:_N_DK + lo + 128] * cd
                + proj[:, _N_DKP + lo:_N_DKP + lo + 128] * sd)
        dk_o[:, lo:lo + 128] = dk_j
        dk_tiles.append(dk_j)
    dv_o[...] = proj[:, _N_DV:_N_DV + 512]
    return xn, proj, cqn, ckvn.astype(BF16), dk_tiles


def _pre_prompt_kernel(x_ref, g_ref, wn_ref, gq_ref, gkv_ref, c32_ref, s32_ref, cd_ref, sd_ref,
                       cm_ref, sm_ref, cmt_ref, smt_ref, cdt_ref, sdt_ref,
                       wt_ref, wuqt_ref, wuqpt_ref, wuk_ref, wuvt_ref,
                       ckv_o, kr_o, dk_o, dv_o, kd_o, km_o, vmt_o, qmt_o, qdt_o, qct_o, vdt_o):
    xn, proj, cqn, ckvb, dk_tiles = _pre_common(
        x_ref, g_ref, wn_ref, gq_ref, gkv_ref, c32_ref, s32_ref, cd_ref, sd_ref,
        ckv_o, kr_o, dk_o, dv_o)
    tm = x_ref.shape[0]
    for j in range(4):
        kd_o[:, j * 128:(j + 1) * 128] = dk_tiles[j].astype(BF16)
    kr128 = (proj[:, _N_KR128:_N_KR128 + 128] * cm_ref[...]
             + proj[:, _N_KR128P:_N_KR128P + 128] * sm_ref[...])
    knope = _dot(ckvb, wuk_ref[...])
    for h in range(MLA_HEADS):
        lo = h * HEAD_PAD
        km_o[:, lo:lo + HEAD_PAD] = (knope[:, lo:lo + HEAD_PAD] + kr128).astype(BF16)
    row = lax.broadcasted_iota(jnp.int32, (V_ROWS - MLA_V, tm), 0)
    ones_rows = jnp.where(row == 0, 1.0, 0.0).astype(BF16)
    vmt = _dot_nt(wuvt_ref[...], ckvb)
    projt = _dot_nt(wt_ref[...], xn)
    for h in range(MLA_HEADS):
        vmt_o[h, 0, 0:MLA_V, :] = vmt[h * MLA_V:(h + 1) * MLA_V, :].astype(BF16)
        vmt_o[h, 0, MLA_V:V_ROWS, :] = ones_rows
        vdt_o[h, 0, 0:DIFF_V, :] = projt[1536 + h * DIFF_V:1536 + (h + 1) * DIFF_V, :].astype(BF16)
        vdt_o[h, 0, DIFF_V:V_ROWS, :] = ones_rows
    qmt = _dot_nt(wuqt_ref[...], cqn)
    qmpt = _dot_nt(wuqpt_ref[...], cqn)
    cmt = cmt_ref[...]
    smt = smt_ref[...]
    for h in range(MLA_HEADS):
        lo = h * HEAD_PAD
        qmt_o[lo:lo + HEAD_PAD, :] = (qmt[lo:lo + HEAD_PAD, :] * cmt
                                      + qmpt[lo:lo + HEAD_PAD, :] * smt).astype(BF16)
    cdt = cdt_ref[...]
    sdt = sdt_ref[...]
    for j in range(4):
        lo = j * 128
        qdt_o[lo:lo + 128, :] = (projt[lo:lo + 128, :] * cdt
                                 + projt[512 + lo:512 + lo + 128, :] * sdt).astype(BF16)
    qct_o[...] = (projt[1024:1536, :] * MEM_SCALE).astype(BF16)


def _pre_sample_kernel(x_ref, g_ref, wn_ref, gq_ref, gkv_ref, c32_ref, s32_ref, cd_ref, sd_ref,
                       cm_ref, sm_ref, wuq_ref, wuqp_ref,
                       ckv_o, kr_o, dk_o, dv_o, qm_o, dq_o, mq_o):
    xn, proj, cqn, ckvb, dk_tiles = _pre_common(
        x_ref, g_ref, wn_ref, gq_ref, gkv_ref, c32_ref, s32_ref, cd_ref, sd_ref,
        ckv_o, kr_o, dk_o, dv_o)
    qm = _dot(cqn, wuq_ref[...])
    qmp = _dot(cqn, wuqp_ref[...])
    cm = cm_ref[...] * MLA_SCALE
    sm = sm_ref[...] * MLA_SCALE
    for h in range(MLA_HEADS):
        lo = h * HEAD_PAD
        qm_o[:, lo:lo + HEAD_PAD] = qm[:, lo:lo + HEAD_PAD] * cm + qmp[:, lo:lo + HEAD_PAD] * sm
    cd = cd_ref[...] * DIFF_SCALE
    sd = sd_ref[...] * DIFF_SCALE
    for j in range(4):
        lo = j * 128
        dq_o[:, lo:lo + 128] = (proj[:, _N_DQ + lo:_N_DQ + lo + 128] * cd
                                + proj[:, _N_DQP + lo:_N_DQP + lo + 128] * sd)
    mq_o[...] = proj[:, _N_MQ:_N_MQ + 512] * MEM_SCALE


def _row_spec(tm, w):
    return pl.BlockSpec((tm, w), lambda i: (i, 0))


def _col_spec(h, tm):
    return pl.BlockSpec((h, tm), lambda i: (0, i))


def _pre_prompt(x, p, tabs):
    rows = x.shape[0]
    tm = TK
    nkb = rows // tm
    in_specs = [
        _row_spec(tm, D_MODEL), _full((1, D_MODEL)), _full(p['wn_prompt'].shape),
        _full((1, MLA_Q_RANK)), _full((1, MLA_KV_RANK)),
        _row_spec(tm, 32), _row_spec(tm, 32), _row_spec(tm, 128), _row_spec(tm, 128),
        _row_spec(tm, 128), _row_spec(tm, 128),
        _col_spec(128, tm), _col_spec(128, tm), _col_spec(128, tm), _col_spec(128, tm),
        _full(p['wt'].shape), _full(p['wuqt'].shape), _full(p['wuqpt'].shape),
        _full(p['wuk_pad'].shape), _full(p['wuvt'].shape),
    ]
    vt_spec = pl.BlockSpec((8, 1, V_ROWS, tm), lambda i: (0, i, 0, 0))
    out_specs = [
        _row_spec(tm, 256), _row_spec(tm, 32), _row_spec(tm, 512), _row_spec(tm, 512),
        _row_spec(tm, 512), _row_spec(tm, 1024), vt_spec,
        _col_spec(1024, tm), _col_spec(512, tm), _col_spec(512, tm), vt_spec,
    ]
    out_shape = [
        jax.ShapeDtypeStruct((rows, 256), F32), jax.ShapeDtypeStruct((rows, 32), F32),
        jax.ShapeDtypeStruct((rows, 512), F32), jax.ShapeDtypeStruct((rows, 512), F32),
        jax.ShapeDtypeStruct((rows, 512), BF16), jax.ShapeDtypeStruct((rows, 1024), BF16),
        jax.ShapeDtypeStruct((8, nkb, V_ROWS, tm), BF16),
        jax.ShapeDtypeStruct((1024, rows), BF16), jax.ShapeDtypeStruct((512, rows), BF16),
        jax.ShapeDtypeStruct((512, rows), BF16),
        jax.ShapeDtypeStruct((8, nkb, V_ROWS, tm), BF16),
    ]
    return pl.pallas_call(
        _pre_prompt_kernel,
        grid=(rows // tm,),
        in_specs=in_specs, out_specs=out_specs, out_shape=out_shape,
        compiler_params=pltpu.CompilerParams(dimension_semantics=("parallel",),
                                             vmem_limit_bytes=VMEM_LIMIT),
        name="pre_prompt",
    )(x, p['pre_mix_g'], p['wn_prompt'], p['gq'], p['gkv'],
      tabs['c32'], tabs['s32'], tabs['cd'], tabs['sd'], tabs['cm'], tabs['sm'],
      tabs['cmt'], tabs['smt'], tabs['cdt'], tabs['sdt'],
      p['wt'], p['wuqt'], p['wuqpt'], p['wuk_pad'], p['wuvt'])


def _pre_sample(x, p, tabs):
    rows = x.shape[0]
    tm = 256
    in_specs = [
        _row_spec(tm, D_MODEL), _full((1, D_MODEL)), _full(p['wn_sample'].shape),
        _full((1, MLA_Q_RANK)), _full((1, MLA_KV_RANK)),
        _row_spec(tm, 32), _row_spec(tm, 32), _row_spec(tm, 128), _row_spec(tm, 128),
        _row_spec(tm, 128), _row_spec(tm, 128),
        _full(p['wuq_pad'].shape), _full(p['wuqp_pad'].shape),
    ]
    out_specs = [
        _row_spec(tm, 256), _row_spec(tm, 32), _row_spec(tm, 512), _row_spec(tm, 512),
        _row_spec(tm, 1024), _row_spec(tm, 512), _row_spec(tm, 512),
    ]
    out_shape = [
        jax.ShapeDtypeStruct((rows, 256), F32), jax.ShapeDtypeStruct((rows, 32), F32),
        jax.ShapeDtypeStruct((rows, 512), F32), jax.ShapeDtypeStruct((rows, 512), F32),
        jax.ShapeDtypeStruct((rows, 1024), F32), jax.ShapeDtypeStruct((rows, 512), F32),
        jax.ShapeDtypeStruct((rows, 512), F32),
    ]
    return pl.pallas_call(
        _pre_sample_kernel,
        grid=(rows // tm,),
        in_specs=in_specs, out_specs=out_specs, out_shape=out_shape,
        compiler_params=pltpu.CompilerParams(dimension_semantics=("parallel",),
                                             vmem_limit_bytes=VMEM_LIMIT),
        name="pre_sample",
    )(x, p['pre_mix_g'], p['wn_sample'], p['gq'], p['gkv'],
      tabs['c32'], tabs['s32'], tabs['cd'], tabs['sd'], tabs['cm'], tabs['sm'],
      p['wuq_pad'], p['wuqp_pad'])


def _chunk_mask(i, j, reps):
    kc = (j * TK + lax.broadcasted_iota(jnp.int32, (TK, TQ), 0)) // CHUNK
    qc = (i * TQ + lax.broadcasted_iota(jnp.int32, (TK, TQ), 1)) // CHUNK
    mask = kc <= qc
    return mask if reps == 1 else jnp.concatenate([mask] * reps, axis=1)


def _flash_blocks(i, qk, process):
    qk(0, 0)

    def body(jj, c):
        j = 2 * jj
        qk(j + 1, 1)
        process(j, 0, False)
        qk(j + 2, 0)
        process(j + 1, 1, False)
        return c

    lax.fori_loop(0, i, body, 0)
    qk(2 * i + 1, 1)
    process(2 * i, 0, True)
    process(2 * i + 1, 1, True)


def _online_update(st, j, h, m_ref, acc_ref, vt_ref):
    m_old = m_ref[h]
    m_new = jnp.maximum(m_old, jnp.max(st, axis=0, keepdims=True))
    m_ref[h] = m_new
    p = jnp.exp(st - m_new).astype(BF16)
    acc_ref[h] = jnp.exp(m_old - m_new) * acc_ref[h] + _dot(vt_ref[h, j], p)


def _kblock(j):
    return pl.ds(pl.multiple_of(j * TK, TK), TK)


def _attn_mla_kernel(qt_ref, k_ref, vt_ref, o_ref, s_ref, m_ref, acc_ref):
    i = pl.program_id(1)
    heads = qt_ref.shape[0] // HEAD_PAD
    m_ref[...] = jnp.full(m_ref.shape, NEG_INF, F32)
    acc_ref[...] = jnp.zeros(acc_ref.shape, F32)

    def qk(j, slot):
        for h in range(heads):
            hs = slice(h * HEAD_PAD, (h + 1) * HEAD_PAD)
            s_ref[slot, h] = _dot(k_ref[_kblock(j), hs], qt_ref[hs, :])

    def process(j, slot, masked):
        for h in range(heads):
            st = s_ref[slot, h]
            if masked:
                st = jnp.where(_chunk_mask(i, j, 1), st, NEG_INF)
            _online_update(st, j, h, m_ref, acc_ref, vt_ref)

    _flash_blocks(i, qk, process)
    for h in range(heads):
        acc = acc_ref[h]
        o_ref[h * MLA_V:(h + 1) * MLA_V, :] = (
            acc[0:MLA_V, :] / acc[MLA_V:MLA_V + 1, :]).astype(BF16)


def _attn_mla(qmt, km, vmt):
    t = km.shape[0]
    hg = 2
    groups = MLA_HEADS // hg
    nq = t // TQ
    return pl.pallas_call(
        _attn_mla_kernel,
        grid=(groups, nq),
        in_specs=[
            pl.BlockSpec((hg * HEAD_PAD, TQ), lambda g, i: (g, i)),
            pl.BlockSpec((t, hg * HEAD_PAD), lambda g, i: (0, g)),
            pl.BlockSpec((hg, t // TK, V_ROWS, TK), lambda g, i: (g, 0, 0, 0)),
        ],
        out_specs=pl.BlockSpec((hg * MLA_V, TQ), lambda g, i: (g, i)),
        out_shape=jax.ShapeDtypeStruct((MLA_HEADS * MLA_V, t), BF16),
        scratch_shapes=[pltpu.VMEM((2, hg, TK, TQ), F32), pltpu.VMEM((hg, 1, TQ), F32),
                        pltpu.VMEM((hg, V_ROWS, TQ), F32)],
        compiler_params=pltpu.CompilerParams(dimension_semantics=("parallel", "parallel"),
                                             vmem_limit_bytes=VMEM_LIMIT),
        name="attn_mla",
    )(qmt, km, vmt)


def _attn_diff_kernel(lam_init, qt_ref, k_ref, vt_ref, lq1_ref, lk1_ref, lq2_ref, lk2_ref,
                      gsub_ref, o_ref, wq_ref, s_ref, m_ref, acc_ref):
    i = pl.program_id(1)
    qt = qt_ref[...]
    grp = lax.broadcasted_iota(jnp.int32, (128, TQ), 0) // DIFF_DC
    zero = jnp.zeros_like(qt)
    for s in range(4):
        wq_ref[:, s * TQ:(s + 1) * TQ] = jnp.where(grp == s, qt, zero)
    m_ref[...] = jnp.full(m_ref.shape, NEG_INF, F32)
    acc_ref[...] = jnp.zeros(acc_ref.shape, F32)

    def qk(j, slot):
        k = k_ref[_kblock(j), :]
        for h in range(2):
            s_ref[slot, h] = _dot(k, wq_ref[:, h * 2 * TQ:(h + 1) * 2 * TQ])

    def process(j, slot, masked):
        for h in range(2):
            st = s_ref[slot, h]
            if masked:
                st = jnp.where(_chunk_mask(i, j, 2), st, NEG_INF)
            _online_update(st, j, h, m_ref, acc_ref, vt_ref)

    _flash_blocks(i, qk, process)

    lam =(jnp.exp(jnp.sum(lq1_ref[...] * lk1_ref[...], axis=-1, keepdims=True))
           - jnp.exp(jnp.sum(lq2_ref[...] * lk2_ref[...], axis=-1, keepdims=True))
           + lam_init)
    g = gsub_ref[...] * (1.0 - lam_init)
    for h in range(2):
        acc = acc_ref[h]
        o0 = acc[0:DIFF_V, 0:TQ] / acc[DIFF_V:DIFF_V + 1, 0:TQ]
        o1 = acc[0:DIFF_V, TQ:2 * TQ] / acc[DIFF_V:DIFF_V + 1, TQ:2 * TQ]
        o = o0 - lam * o1
        o = o * lax.rsqrt(jnp.mean(o * o, axis=0, keepdims=True) + EPS) * g
        o_ref[h * DIFF_V:(h + 1) * DIFF_V, :] = o.astype(BF16)


def _attn_diff(qdt, kd, vdt, lq1, lk1, lq2, lk2, gsub, lam_init):
    t = kd.shape[0]
    nq = t // TQ
    pairs = DIFF_HEADS // 2
    vec = _full((1, DIFF_DC))
    return pl.pallas_call(
        functools.partial(_attn_diff_kernel, lam_init),
        grid=(pairs, nq),
        in_specs=[
            pl.BlockSpec((128, TQ), lambda g, i: (g, i)),
            pl.BlockSpec((t, 128), lambda g, i: (0, g)),
            pl.BlockSpec((2, t // TK, V_ROWS, TK), lambda g, i: (g, 0, 0, 0)),
            vec, vec, vec, vec, _full((DIFF_V, 1)),
        ],
        out_specs=pl.BlockSpec((2 * DIFF_V, TQ), lambda g, i: (g, i)),
        out_shape=jax.ShapeDtypeStruct((DIFF_HEADS * DIFF_V, t), BF16),
        scratch_shapes=[pltpu.VMEM((128, 4 * TQ), BF16), pltpu.VMEM((2, 2, TK, 2 * TQ), F32),
                        pltpu.VMEM((2, 1, 2 * TQ), F32), pltpu.VMEM((2, V_ROWS, 2 * TQ), F32)],
        compiler_params=pltpu.CompilerParams(dimension_semantics=("parallel", "parallel"),
                                             vmem_limit_bytes=VMEM_LIMIT),
        name="attn_diff",
    )(qdt, kd, vdt, lq1, lk1, lq2, lk2, gsub)


def _attn_mem_kernel(qt_ref, k_ref, vt_ref, o_ref):
    for h in range(MEM_HEADS):
        sl = slice(h * MEM_DH, (h + 1) * MEM_DH)
        st = _dot(k_ref[:, sl], qt_ref[sl, :])
        e = jnp.exp(st - jnp.max(st, axis=0, keepdims=True))
        p = (e / jnp.sum(e, axis=0, keepdims=True)).astype(BF16)
        o_ref[sl, :] = _dot(vt_ref[sl, :], p).astype(BF16)


def _attn_mem(qct, mk, mvt):
    w, t = qct.shape
    tq = 512
    return pl.pallas_call(
        _attn_mem_kernel,
        grid=(t // tq,),
        in_specs=[pl.BlockSpec((w, tq), lambda i: (0, i)), _full(mk.shape), _full(mvt.shape)],
        out_specs=pl.BlockSpec((w, tq), lambda i: (0, i)),
        out_shape=jax.ShapeDtypeStruct((w, t), BF16),
        compiler_params=pltpu.CompilerParams(dimension_semantics=("parallel",)),
        name="attn_mem",
    )(qct, mk, mvt)


def _softmax_two(s_a, s_b):
    m = jnp.maximum(jnp.max(s_a, axis=-1, keepdims=True), jnp.max(s_b, axis=-1, keepdims=True))
    e_a = jnp.exp(s_a - m)
    e_b = jnp.exp(s_b - m)
    inv = 1.0 / (jnp.sum(e_a, axis=-1, keepdims=True) + jnp.sum(e_b, axis=-1, keepdims=True))
    return e_a * inv, e_b * inv


def _diag_blocks(o_all, rows, width, nblk):
    lane_blk = lax.broadcasted_iota(jnp.int32, (rows, nblk * width), 1) // width
    out = jnp.zeros((rows, nblk * width), F32)
    for b in range(nblk):
        out = out + jnp.where(lane_blk == b, o_all[b * rows:(b + 1) * rows, :], 0.0)
    return out


def _sample_mla_kernel(qm_ref, ckvn_ref, krn_ref, mq_ref, ckvp_ref, krp_ref, mk_ref, mv_ref,
                       wabs_ref, wuv_ref, oa_ref, oc_ref):
    nq = qm_ref.shape[0]
    qm = qm_ref[...].astype(BF16)
    qext = jnp.concatenate(
        [_dot(qm[:, h * HEAD_PAD:(h + 1) * HEAD_PAD], wabs_ref[h]) for h in range(MLA_HEADS)],
        axis=0).astype(BF16)
    q_lat = qext[:, 0:MLA_KV_RANK]
    q_rope = qext[:, MLA_KV_RANK:MLA_KV_RANK + MLA_ROPE]
    ckv_p = ckvp_ref[...].astype(BF16)
    ckv_n = ckvn_ref[...].astype(BF16)
    s_p = _dot_nt(q_lat, ckv_p) + _dot_nt(q_rope, krp_ref[...].astype(BF16))
    s_n = _dot_nt(q_lat, ckv_n) + _dot_nt(q_rope, krn_ref[...].astype(BF16))
    p_p, p_n = _softmax_two(s_p, s_n)
    o_lat = _dot(p_p.astype(BF16), ckv_p) + _dot(p_n.astype(BF16), ckv_n)
    o_all = _dot(o_lat.astype(BF16), wuv_ref[...])
    oa_ref[...] = _diag_blocks(o_all, nq, MLA_V, MLA_HEADS)
    mq = mq_ref[...].astype(BF16)
    for h in range(MEM_HEADS):
        sl = slice(h * MEM_DH, (h + 1) * MEM_DH)
        s = _dot_nt(mq[:, sl], mk_ref[:, sl].astype(BF16))
        e = jnp.exp(s - jnp.max(s, axis=-1, keepdims=True))
        p = (e / jnp.sum(e, axis=-1, keepdims=True)).astype(BF16)
        oc_ref[:, sl] = _dot(p, mv_ref[:, sl].astype(BF16))


def _sample_mla(qm, ckv_new, kr_new, mq, ckv_past, kr_past, mem_k, mem_v, wabs, wuv, nb, nq):
    def rows(w):
        return pl.BlockSpec((nq, w), lambda b: (b, 0))

    def cache(shape):
        return pl.BlockSpec((None,) + shape, lambda b: (b, 0, 0))

    past = ckv_past.shape[1]
    return pl.pallas_call(
        _sample_mla_kernel,
        grid=(nb,),
        in_specs=[rows(1024), rows(256), rows(32), rows(512),
                  cache((past, 256)), cache((past, 32)), cache(mem_k.shape[1:]),
                  cache(mem_v.shape[1:]), _full(wabs.shape), _full(wuv.shape)],
        out_specs=[rows(512), rows(512)],
        out_shape=[jax.ShapeDtypeStruct((nb * nq, 512), F32)] * 2,
        compiler_params=pltpu.CompilerParams(dimension_semantics=("parallel",),
                                             vmem_limit_bytes=VMEM_LIMIT),
        name="sample_mla_mem",
    )(qm, ckv_new, kr_new, mq, ckv_past, kr_past, mem_k, mem_v, wabs, wuv)


def _sample_diff_kernel(lam_init, dq_ref, dkn_ref, dvn_ref, dkp_ref, dvp_ref,
                        lq1_ref, lk1_ref, lq2_ref, lk2_ref, o_ref):
    nq = dq_ref.shape[0]
    w = dq_ref.shape[1]
    nmaps = w // DIFF_DC
    q = dq_ref[...]
    row_grp = lax.broadcasted_iota(jnp.int32, (nmaps * nq, w), 0) // nq
    lane_grp = lax.broadcasted_iota(jnp.int32, (nmaps * nq, w), 1) // DIFF_DC
    qbd = jnp.where(row_grp == lane_grp, jnp.concatenate([q] * nmaps, axis=0), 0.0).astype(BF16)
    k_p = dkp_ref[...].astype(BF16)
    k_n = dkn_ref[...].astype(BF16)
    p_p, p_n = _softmax_two(_dot_nt(qbd, k_p), _dot_nt(qbd, k_n))
    lam = (jnp.exp(jnp.sum(lq1_ref[...] * lk1_ref[...], axis=-1, keepdims=True))
           - jnp.exp(jnp.sum(lq2_ref[...] * lk2_ref[...], axis=-1, keepdims=True))
           + lam_init)

    def combine(p):
        parts = [p[(2 * h) * nq:(2 * h + 1) * nq, :] - lam * p[(2 * h + 1) * nq:(2 * h + 2) * nq, :]
                 for h in range(nmaps // 2)]
        return jnp.concatenate(parts, axis=0).astype(BF16)

    o_all = (_dot(combine(p_p), dvp_ref[...].astype(BF16))
             + _dot(combine(p_n), dvn_ref[...].astype(BF16)))
    o_ref[...] = _diag_blocks(o_all, nq, DIFF_V, nmaps // 2)


def _sample_diff(dq, dk_new, dv_new, dk_past, dv_past, lq1, lk1, lq2, lk2, lam_init, nb, nq):
    hw = 256
    past = dk_past.shape[1]
    rows = pl.BlockSpec((nq, hw), lambda b, g: (b, g))
    cache = pl.BlockSpec((None, past, hw), lambda b, g: (b, 0, g))
    vec = _full((1, DIFF_DC))
    return pl.pallas_call(
        functools.partial(_sample_diff_kernel, lam_init),
        grid=(nb, 512 // hw),
        in_specs=[rows, rows, rows, cache, cache, vec, vec, vec, vec],
        out_specs=rows,
        out_shape=jax.ShapeDtypeStruct((nb * nq, 512), F32),
        compiler_params=pltpu.CompilerParams(dimension_semantics=("parallel", "parallel"),
                                             vmem_limit_bytes=VMEM_LIMIT),
        name="sample_diff",
    )(dq, dk_new, dv_new, dk_past, dv_past, lq1, lk1, lq2, lk2)


def _subln_kernel(lam_init, o_ref, g_ref, out_ref):
    g = g_ref[...] * (1.0 - lam_init)
    for h in range(DIFF_HEADS):
        o = o_ref[h * DIFF_V:(h + 1) * DIFF_V, :]
        out_ref[h * DIFF_V:(h + 1) * DIFF_V, :] = (
            o * lax.rsqrt(jnp.mean(o * o, axis=0, keepdims=True) + EPS) * g).astype(BF16)


def _subln(ot, gsub, lam_init):
    return pl.pallas_call(
        functools.partial(_subln_kernel, lam_init),
        out_shape=jax.ShapeDtypeStruct(ot.shape, BF16),
        name="sample_subln",
    )(ot, gsub)


def _mix_kernel(x_ref, oa_ref, ob_ref, oc_ref, g_ref, wg_ref, bg_ref, woa_ref, wob_ref, woc_ref,
                wout_ref, gpost_ref, y_ref):
    x = x_ref[...]
    xn = _rms(x, g_ref[...]).astype(BF16)
    d = x.shape[1]
    merged = None
    for b, (o_ref, w_ref) in enumerate(((oa_ref, woa_ref), (ob_ref, wob_ref), (oc_ref, woc_ref))):
        gate = jax.nn.sigmoid(_dot(xn, wg_ref[:, b * d:(b + 1) * d]) + bg_ref[:, b * d:(b + 1) * d])
        term = gate * _dot_tn(o_ref[...], w_ref[...])
        merged = term if merged is None else merged + term
    mix = _dot(merged.astype(BF16), wout_ref[...])
    y_ref[...] = x + _rms(mix, gpost_ref[...])


def _mix(x, oat, obt, oct, p):
    rows, d = x.shape
    tm = 512 if rows % 512 == 0 else rows
    ot_spec = pl.BlockSpec((512, tm), lambda i: (0, i))
    return pl.pallas_call(
        _mix_kernel,
        grid=(rows // tm,),
        in_specs=[_row_spec(tm, d), ot_spec, ot_spec, ot_spec, _full((1, d)),
                  _full(p['w_gate'].shape), _full((1, 3 * d)), _full((512, d)), _full((512, d)),
                  _full((512, d)), _full((d, d)), _full((1, d))],
        out_specs=_row_spec(tm, d),
        out_shape=jax.ShapeDtypeStruct((rows, d), F32),
        compiler_params=pltpu.CompilerParams(dimension_semantics=("parallel",),
                                             vmem_limit_bytes=VMEM_LIMIT),
        name="mix",
    )(x, oat, obt, oct, p['pre_mix_g'], p['w_gate'], p['b_gate'], p['w_o_mla'], p['w_o_diff'],
      p['w_o_mem'], p['w_out'], p['post_mix_g'])


def _mlp_kernel(x_ref, g_ref, wup_ref, wdn_ref, gpost_ref, y_ref):
    x = x_ref[...]
    h = _rms(x, g_ref[...]).astype(BF16)
    u = jnp.maximum(_dot(h, wup_ref[...]), 0.0)
    f = _dot((u * u).astype(BF16), wdn_ref[...])
    y_ref[...] = x + _rms(f, gpost_ref[...])


def _mlp(x, p):
    rows, d = x.shape
    tm = 256
    return pl.pallas_call(
        _mlp_kernel,
        grid=(rows // tm,),
        in_specs=[_row_spec(tm, d), _full((1, d)), _full(p['w_mlp_up'].shape),
                  _full(p['w_mlp_down'].shape), _full((1, d))],
        out_specs=_row_spec(tm, d),
        out_shape=jax.ShapeDtypeStruct((rows, d), F32),
        compiler_params=pltpu.CompilerParams(dimension_semantics=("parallel",),
                                             vmem_limit_bytes=VMEM_LIMIT),
        name="mlp",
    )(x, p['pre_mlp_g'], p['w_mlp_up'], p['w_mlp_down'], p['post_mlp_g'])


def _partner(width, group, half):
    idx = np.arange(width)
    sign = np.zeros(width, np.float32)
    d = idx % group
    first = d < half
    second = (d >= half) & (d < 2 * half)
    src = np.where(first, idx + half, np.where(second, idx - half, idx))
    sign[first] = -1.0
    sign[second] = 1.0
    return src, sign


def _take_signed(w, src, sign):
    return w[:, src] * jnp.asarray(sign)[None, :]


def _pad_cols(w, total):
    return jnp.pad(w, ((0, 0), (0, total - w.shape[1])))


def _prep_layer(l, w_in, mla_w_uq, mla_w_uk, mla_w_uv, w_mem_k, w_mem_v, w_o_mla, w_o_diff, w_o_mem,
                w_gate, b_gate, w_out, w_mlp_up, w_mlp_down, gains):
    cq, ckv, kr, dq, dk, dv, mq = jnp.split(
        w_in[l], np.cumsum((384, 256, 32, 512, 512, 512))[...].tolist(), axis=1)
    src_d, sign_d = _partner(512, DIFF_DC, DIFF_ROT // 2)
    src_r, sign_r = _partner(32, 32, MLA_ROPE // 2)
    dkp = _take_signed(dk, src_d, sign_d)
    dqp = _take_signed(dq, src_d, sign_d)
    krp = _take_signed(kr, src_r, sign_r)
    common = [cq, ckv, dk, dkp, dv, _pad_cols(kr, 128), _pad_cols(krp, 128)]
    zeros64 = jnp.zeros((D_MODEL, MLA_NOPE), F32)
    kr128 = _pad_cols(jnp.concatenate([zeros64, kr], axis=1), 128)
    kr128p = _pad_cols(jnp.concatenate([zeros64, krp], axis=1), 128)
    head_w = MLA_NOPE + MLA_ROPE
    e = np.arange(HEAD_PAD)
    valid = e < head_w
    src_q = np.concatenate([h * head_w + np.where(valid, e, 0) for h in range(MLA_HEADS)])
    sign_q = np.tile(valid.astype(np.float32), MLA_HEADS)
    in_rope1 = (e >= MLA_NOPE) & (e < MLA_NOPE + MLA_ROPE // 2)
    in_rope2 = (e >= MLA_NOPE + MLA_ROPE // 2) & valid
    pe = np.where(in_rope1, e + MLA_ROPE // 2, np.where(in_rope2, e - MLA_ROPE // 2, 0))
    src_qp = np.concatenate([h * head_w + pe for h in range(MLA_HEADS)])
    sign_qp = np.tile(np.where(in_rope1, -1.0, np.where(in_rope2, 1.0, 0.0)).astype(np.float32),
                      MLA_HEADS)
    wuq_pad = _take_signed(mla_w_uq[l], src_q, sign_q)
    wuqp_pad = _take_signed(mla_w_uq[l], src_qp, sign_qp)
    wuk = mla_w_uk[l]
    wuk_pad = jnp.pad(wuk, ((0, 0), (0, 0), (0, HEAD_PAD - MLA_NOPE))).reshape(MLA_KV_RANK, -1)
    wuv = mla_w_uv[l].reshape(MLA_KV_RANK, MLA_HEADS * MLA_V)
    sel = np.zeros((HEAD_PAD, 128), np.float32)
    sel[MLA_NOPE + np.arange(MLA_ROPE), np.arange(MLA_ROPE)] = 1.0
    wabs = jnp.concatenate([
        jnp.pad(jnp.transpose(wuk, (1, 2, 0)), ((0, 0), (0, HEAD_PAD - MLA_NOPE), (0, 0))),
        jnp.broadcast_to(jnp.asarray(sel), (MLA_HEADS, HEAD_PAD, 128))], axis=2)
    bf = lambda a: a.astype(BF16)
    row = lambda a: a[l][None, :]
    p = {
        'wn_prompt': bf(jnp.concatenate(common + [kr128, kr128p], axis=1)),
        'wn_sample': bf(jnp.concatenate(common + [dq, dqp, mq], axis=1)),
        'wt': bf(jnp.concatenate([dq, dqp, mq, dv], axis=1).T),
        'wuqt': bf(wuq_pad.T), 'wuqpt': bf(wuqp_pad.T),
        'wuq_pad': bf(wuq_pad), 'wuqp_pad': bf(wuqp_pad),
        'wuk_pad': bf(wuk_pad), 'wuvt': bf(wuv.T), 'wuv': bf(wuv), 'wabs': bf(wabs),
        'w_mem_k': bf(w_mem_k[l]), 'w_mem_v': bf(w_mem_v[l]), 'w_mem_vt': bf(w_mem_v[l].T),
        'w_o_mla': bf(w_o_mla[l]), 'w_o_diff': bf(w_o_diff[l]), 'w_o_mem': bf(w_o_mem[l]),
        'w_gate': bf(w_gate[l]), 'b_gate': row(b_gate), 'w_out': bf(w_out[l]),
        'w_mlp_up': bf(w_mlp_up[l]), 'w_mlp_down': bf(w_mlp_down[l]),
    }
    for name, g in gains.items():
        p[name] = row(g)
    return p


def _rope_cos_sin(pos, rot_dim, theta):
    half = rot_dim // 2
    inv = jnp.power(jnp.float32(theta), -jnp.arange(half, dtype=F32) * (2.0 / rot_dim))
    ang = pos.astype(F32)[:, None] * inv[None, :]
    return jnp.cos(ang), jnp.sin(ang)


def _tables(pos, reps):
    n = pos.shape[0]
    cm, sm = _rope_cos_sin(pos, MLA_ROPE, MLA_THETA)
    cd, sd = _rope_cos_sin(pos, DIFF_ROT, ROPE_THETA)
    one = lambda w: jnp.ones((n, w), F32)
    zero = lambda w: jnp.zeros((n, w), F32)
    t = {
        'c32': jnp.concatenate([cm, cm], axis=1), 's32': jnp.concatenate([sm, sm], axis=1),
        'cm': jnp.concatenate([one(64), cm, cm, one(32)], axis=1),
        'sm': jnp.concatenate([zero(64), sm, sm, zero(32)], axis=1),
        'cd': jnp.tile(jnp.concatenate([cd, cd, one(24)], axis=1), (1, 4)),
        'sd': jnp.tile(jnp.concatenate([sd, sd, zero(24)], axis=1), (1, 4)),
    }
    return {k: jnp.tile(v, (reps, 1)) for k, v in t.items()}


def kernel(x_prompt, x_sample, cache_mla_ckv, cache_mla_krope, cache_diff_k, cache_diff_v, cache_mem_k, cache_mem_v, mem_prompt, pre_mix_g, w_in, mla_q_norm_g, mla_w_uq, mla_kv_norm_g, mla_w_uk, mla_w_uv, diff_lq1, diff_lk1, diff_lq2, diff_lk2, diff_subln_g, mem_norm_g, w_mem_k, w_mem_v, w_o_mla, w_o_diff, w_o_mem, w_gate, b_gate, w_out, post_mix_g, pre_mlp_g, w_mlp_up, w_mlp_down, post_mlp_g):
    depth = w_in.shape[0]
    bp, t, d = x_prompt.shape
    nb, nq, _ = x_sample.shape
    past = cache_mla_ckv.shape[2]
    assert bp == 1 and t % TQ == 0 and TQ == 2 * TK and TK % CHUNK == 0 and d == D_MODEL
    assert past % CHUNK == 0 and nq <= CHUNK

    tabs_p = _tables(jnp.arange(t, dtype=jnp.int32), 1)
    tabs_p['cmt'] = (tabs_p['cm'] * MLA_SCALE).T
    tabs_p['smt'] = (tabs_p['sm'] * MLA_SCALE).T
    tabs_p['cdt'] = (tabs_p['cd'] * DIFF_SCALE).T
    tabs_p['sdt'] = (tabs_p['sd'] * DIFF_SCALE).T
    tabs_s = _tables(past + jnp.arange(nq, dtype=jnp.int32), nb)

    xp = x_prompt.reshape(t, d)
    xs = x_sample.reshape(nb * nq, d)
    outs = {k: [] for k in ('p_ckv', 'p_kr', 'p_dk', 'p_dv', 'p_mk', 'p_mv',
                            's_ckv', 's_kr', 's_dk', 's_dv')}
    for l in range(depth):
        lam_init = 0.8 - 0.6 * math.exp(-0.3 * l)
        p = _prep_layer(l, w_in, mla_w_uq, mla_w_uk, mla_w_uv, w_mem_k, w_mem_v, w_o_mla, w_o_diff,
                        w_o_mem, w_gate, b_gate, w_out, w_mlp_up, w_mlp_down,
                        {'pre_mix_g': pre_mix_g, 'gq': mla_q_norm_g, 'gkv': mla_kv_norm_g,
                         'mem_norm_g': mem_norm_g, 'post_mix_g': post_mix_g,
                         'pre_mlp_g': pre_mlp_g, 'post_mlp_g': post_mlp_g})
        lq1, lk1, lq2, lk2 = (a[l][None, :] for a in (diff_lq1, diff_lk1, diff_lq2, diff_lk2))
        gsub = diff_subln_g[l][:, None]

        mk, mv, mkb, mvt = _memkv(mem_prompt[0], p['mem_norm_g'], p['w_mem_k'], p['w_mem_v'],
                                  p['w_mem_vt'])
        (ckv_p, kr_p, dk_p, dv_p, kd, km, vmt, qmt, qdt, qct, vdt) = _pre_prompt(xp, p, tabs_p)
        oat = _attn_mla(qmt, km, vmt)
        obt = _attn_diff(qdt, kd, vdt, lq1, lk1, lq2, lk2, gsub, lam_init)
        oct = _attn_mem(qct, mkb, mvt)
        xp = _mlp(_mix(xp, oat, obt, oct, p), p)

        (ckv_s, kr_s, dk_s, dv_s, qm_s, dq_s, mq_s) = _pre_sample(xs, p, tabs_s)
        oa_s, oc_s = _sample_mla(qm_s, ckv_s, kr_s, mq_s, cache_mla_ckv[l], cache_mla_krope[l],
                                 cache_mem_k[l].reshape(nb, -1, 512),
                                 cache_mem_v[l].reshape(nb, -1, 512), p['wabs'], p['wuv'], nb, nq)
        ob_s = _sample_diff(dq_s, dk_s, dv_s, cache_diff_k[l].reshape(nb, past, 512),
                            cache_diff_v[l].reshape(nb, past, 512), lq1, lk1, lq2, lk2,
                            lam_init, nb, nq)
        obt_s = _subln(ob_s.T, gsub, lam_init)
        xs = _mlp(_mix(xs, oa_s.T.astype(BF16), obt_s, oc_s.T.astype(BF16), p), p)

        outs['p_ckv'].append(ckv_p.reshape(1, t, MLA_KV_RANK))
        outs['p_kr'].append(kr_p.reshape(1, t, MLA_ROPE))
        outs['p_dk'].append(dk_p.reshape(1, t, DIFF_HEADS, DIFF_V))
        outs['p_dv'].append(dv_p.reshape(1, t, DIFF_HEADS, DIFF_V))
        outs['p_mk'].append(mk.reshape(1, -1, MEM_HEADS, MEM_DH))
        outs['p_mv'].append(mv.reshape(1, -1, MEM_HEADS, MEM_DH))
        outs['s_ckv'].append(ckv_s.reshape(nb, nq, MLA_KV_RANK))
        outs['s_kr'].append(kr_s.reshape(nb, nq, MLA_ROPE))
        outs['s_dk'].append(dk_s.reshape(nb, nq, DIFF_HEADS, DIFF_V))
        outs['s_dv'].append(dv_s.reshape(nb, nq, DIFF_HEADS, DIFF_V))

    st = lambda k: jnp.stack(outs[k], axis=0)
    return (xp.reshape(1, t, d), xs.reshape(nb, nq, d),
            st('p_ckv'), st('p_kr'), st('p_dk'), st('p_dv'), st('p_mk'), st('p_mv'),
            st('s_ckv'), st('s_kr'), st('s_dk'), st('s_dv'))
```

```python
import functools
import math

import numpy as np
import jax
import jax.numpy as jnp
from jax import lax
from jax.experimental import pallas as pl
from jax.experimental.pallas import tpu as pltpu

F32 = jnp.float32
BF16 = jnp.bfloat16

D_MODEL = 1024
CHUNK = 64
EPS = 1e-6
NEG_INF = -1e30
MLA_HEADS = 8
MLA_Q_RANK = 384
MLA_KV_RANK = 256
MLA_NOPE = 64
MLA_ROPE = 32
MLA_V = 64
MLA_THETA = 10000.0
MLA_SCALE = (MLA_NOPE + MLA_ROPE) ** -0.5
DIFF_HEADS = 8
DIFF_DC = 32
DIFF_V = 64
DIFF_ROT = 8
ROPE_THETA = 500000.0
DIFF_SCALE = DIFF_DC ** -0.5
MEM_HEADS = 4
MEM_DH = 128
MEM_SCALE = MEM_DH ** -0.5
LOG2E = math.log2(math.e)
HEAD_PAD = 128
V_ROWS = 80
TQ = 512
TK = 256
ATTN_HEADS_PER_STEP = 4
VMEM_LIMIT = 56 * 1024 * 1024

_NT = (((1,), (1,)), ((), ()))
_TN = (((0,), (0,)), ((), ()))


def _dot(a, b):
    return jnp.dot(a, b, preferred_element_type=F32)


def _dot_nt(a, b):
    return lax.dot_general(a, b, _NT, preferred_element_type=F32)


def _dot_tn(a, b):
    return lax.dot_general(a, b, _TN, preferred_element_type=F32)


def _rms(x, g):
    return x * lax.rsqrt(jnp.mean(x * x, axis=-1, keepdims=True) + EPS) * g


def _full(shape):
    nd = len(shape)
    return pl.BlockSpec(shape, lambda *_: (0,) * nd)


def _memkv_kernel(mem_ref, g_ref, wk_ref, wv_ref, wvt_ref, k_ref, v_ref, kb_ref, vt_ref):
    mn = _rms(mem_ref[...], g_ref[...]).astype(BF16)
    k = _dot(mn, wk_ref[...])
    k_ref[...] = k
    kb_ref[...] = k.astype(BF16)
    v_ref[...] = _dot(mn, wv_ref[...])
    vt_ref[...] = _dot_nt(wvt_ref[...], mn).astype(BF16)


def _memkv(mem, g, wk, wv, wvt):
    n_mem = mem.shape[0]
    w = wk.shape[1]
    return pl.pallas_call(
        _memkv_kernel,
        out_shape=(jax.ShapeDtypeStruct((n_mem, w), F32), jax.ShapeDtypeStruct((n_mem, w), F32),
                   jax.ShapeDtypeStruct((n_mem, w), BF16), jax.ShapeDtypeStruct((w, n_mem), BF16)),
        name="memkv",
    )(mem, g, wk, wv, wvt)


_N_CQ, _N_CKV, _N_DK, _N_DKP, _N_DV, _N_KR, _N_KRP = 0, 384, 640, 1152, 1664, 2176, 2304
_N_COMMON = 2432
_N_KR128, _N_KR128P = 2432, 2560
_N_PROMPT = 2688
_N_DQ, _N_DQP, _N_MQ = 2432, 2944, 3456
_N_SAMPLE = 3968


def _pre_common(x_ref, g_ref, wn_ref, gq_ref, gkv_ref, c32_ref, s32_ref, cd_ref, sd_ref,
                ckv_o, kr_o, dk_o, dv_o):
    xn = _rms(x_ref[...], g_ref[...]).astype(BF16)
    proj = _dot(xn, wn_ref[...])
    cqn = _rms(proj[:, _N_CQ:_N_CQ + MLA_Q_RANK], gq_ref[...]).astype(BF16)
    ckvn = _rms(proj[:, _N_CKV:_N_CKV + MLA_KV_RANK], gkv_ref[...])
    ckv_o[...] = ckvn
    kr_o[...] = (proj[:, _N_KR:_N_KR + MLA_ROPE] * c32_ref[...]
                 + proj[:, _N_KRP:_N_KRP + MLA_ROPE] * s32_ref[...])
    cd = cd_ref[...]
    sd = sd_ref[...]
    dk_tiles = []
    for j in range(4):
        lo = j * 128
        dk_j = (proj[:, _N_DK + lo:_N_DK + lo + 128] * cd
                + proj[:, _N_DKP + lo:_N_DKP + lo + 128] * sd)
        dk_o[:, lo:lo + 128] = dk_j
        dk_tiles.append(dk_j)
    dv_o[...] = proj[:, _N_DV:_N_DV + 512]
    return xn, proj, cqn, ckvn.astype(BF16), dk_tiles


def _pre_prompt_kernel(x_ref, g_ref, wn_ref, gq_ref, gkv_ref, c32_ref, s32_ref, cd_ref, sd_ref,
                       cm_ref, sm_ref, cmt_ref, smt_ref, cdt_ref, sdt_ref,
                       wt_ref, wuqt_ref, wuqpt_ref, wuk_ref, wuvt_ref,
                       ckv_o, kr_o, dk_o, dv_o, kd_o, km_o, vmt_o, qmt_o, qdt_o, qct_o, vdt_o):
    xn, proj, cqn, ckvb, dk_tiles = _pre_common(
        x_ref, g_ref, wn_ref, gq_ref, gkv_ref, c32_ref, s32_ref, cd_ref, sd_ref,
        ckv_o, kr_o, dk_o, dv_o)
    tm = x_ref.shape[0]
    for j in range(4):
        kd_o[:, j * 128:(j + 1) * 128] = dk_tiles[j].astype(BF16)
    kr128 = (proj[:, _N_KR128:_N_KR128 + 128] * cm_ref[...]
             + proj[:, _N_KR128P:_N_KR128P + 128] * sm_ref[...])
    knope = _dot(ckvb, wuk_ref[...])
    for h in range(MLA_HEADS):
        lo = h * HEAD_PAD
        km_o[:, lo:lo + HEAD_PAD] = (knope[:, lo:lo + HEAD_PAD] + kr128).astype(BF16)
    row = lax.broadcasted_iota(jnp.int32, (V_ROWS - MLA_V, tm), 0)
    ones_rows = jnp.where(row == 0, 1.0, 0.0).astype(BF16)
    vmt = _dot_nt(wuvt_ref[...], ckvb)
    projt = _dot_nt(wt_ref[...], xn)
    for h in range(MLA_HEADS):
        vmt_o[h, 0, 0:MLA_V, :] = vmt[h * MLA_V:(h + 1) * MLA_V, :].astype(BF16)
        vmt_o[h, 0, MLA_V:V_ROWS, :] = ones_rows
        vdt_o[h, 0, 0:DIFF_V, :] = projt[1536 + h * DIFF_V:1536 + (h + 1) * DIFF_V, :].astype(BF16)
        vdt_o[h, 0, DIFF_V:V_ROWS, :] = ones_rows
    qmt = _dot_nt(wuqt_ref[...], cqn)
    qmpt = _dot_nt(wuqpt_ref[...], cqn)
    cmt = cmt_ref[...]
    smt = smt_ref[...]
    for h in range(MLA_HEADS):
        lo = h * HEAD_PAD
        qmt_o[lo:lo + HEAD_PAD, :] = (qmt[lo:lo + HEAD_PAD, :] * cmt
                                      + qmpt[lo:lo + HEAD_PAD, :] * smt).astype(BF16)
    cdt = cdt_ref[...]
    sdt = sdt_ref[...]
    for j in range(4):
        lo = j * 128
        qdt_o[lo:lo + 128, :] = (projt[lo:lo + 128, :] * cdt
                                 + projt[512 + lo:512 + lo + 128, :] * sdt).astype(BF16)
    qct_o[...] = (projt[1024:1536, :] * MEM_SCALE).astype(BF16)


def _pre_sample_kernel(x_ref, g_ref, wn_ref, gq_ref, gkv_ref, c32_ref, s32_ref, cd_ref, sd_ref,
                       cm_ref, sm_ref, wuq_ref, wuqp_ref,
                       ckv_o, kr_o, dk_o, dv_o, qm_o, dq_o, mq_o):
    xn, proj, cqn, ckvb, dk_tiles = _pre_common(
        x_ref, g_ref, wn_ref, gq_ref, gkv_ref, c32_ref, s32_ref, cd_ref, sd_ref,
        ckv_o, kr_o, dk_o, dv_o)
    qm = _dot(cqn, wuq_ref[...])
    qmp = _dot(cqn, wuqp_ref[...])
    cm = cm_ref[...] * MLA_SCALE
    sm = sm_ref[...] * MLA_SCALE
    for h in range(MLA_HEADS):
        lo = h * HEAD_PAD
        qm_o[:, lo:lo + HEAD_PAD] = qm[:, lo:lo + HEAD_PAD] * cm + qmp[:, lo:lo + HEAD_PAD] * sm
    cd = cd_ref[...] * DIFF_SCALE
    sd = sd_ref[...] * DIFF_SCALE
    for j in range(4):
        lo = j * 128
        dq_o[:, lo:lo + 128] = (proj[:, _N_DQ + lo:_N_DQ + lo + 128] * cd
                                + proj[:, _N_DQP + lo:_N_DQP + lo + 128] * sd)
    mq_o[...] = proj[:, _N_MQ:_N_MQ + 512] * MEM_SCALE


def _row_spec(tm, w):
    return pl.BlockSpec((tm, w), lambda i: (i, 0))


def _col_spec(h, tm):
    return pl.BlockSpec((h, tm), lambda i: (0, i))


def _pre_prompt(x, p, tabs):
    rows = x.shape[0]
    tm = TK
    nkb = rows // tm
    in_specs = [
        _row_spec(tm, D_MODEL), _full((1, D_MODEL)), _full(p['wn_prompt'].shape),
        _full((1, MLA_Q_RANK)), _full((1, MLA_KV_RANK)),
        _row_spec(tm, 32), _row_spec(tm, 32), _row_spec(tm, 128), _row_spec(tm, 128),
        _row_spec(tm, 128), _row_spec(tm, 128),
        _col_spec(128, tm), _col_spec(128, tm), _col_spec(128, tm), _col_spec(128, tm),
        _full(p['wt'].shape), _full(p['wuqt'].shape), _full(p['wuqpt'].shape),
        _full(p['wuk_pad'].shape), _full(p['wuvt'].shape),
    ]
    vt_spec = pl.BlockSpec((8, 1, V_ROWS, tm), lambda i: (0, i, 0, 0))
    out_specs = [
        _row_spec(tm, 256), _row_spec(tm, 32), _row_spec(tm, 512), _row_spec(tm, 512),
        _row_spec(tm, 512), _row_spec(tm, 1024), vt_spec,
        _col_spec(1024, tm), _col_spec(512, tm), _col_spec(512, tm), vt_spec,
    ]
    out_shape = [
        jax.ShapeDtypeStruct((rows, 256), F32), jax.ShapeDtypeStruct((rows, 32), F32),
        jax.ShapeDtypeStruct((rows, 512), F32), jax.ShapeDtypeStruct((rows, 512), F32),
        jax.ShapeDtypeStruct((rows, 512), BF16), jax.ShapeDtypeStruct((rows, 1024), BF16),
        jax.ShapeDtypeStruct((8, nkb, V_ROWS, tm), BF16),
        jax.ShapeDtypeStruct((1024, rows), BF16), jax.ShapeDtypeStruct((512, rows), BF16),
        jax.ShapeDtypeStruct((512, rows), BF16),
        jax.ShapeDtypeStruct((8, nkb, V_ROWS, tm), BF16),
    ]
    return pl.pallas_call(
        _pre_prompt_kernel,
        grid=(rows // tm,),
        in_specs=in_specs, out_specs=out_specs, out_shape=out_shape,
        compiler_params=pltpu.CompilerParams(dimension_semantics=("parallel",),
                                             vmem_limit_bytes=VMEM_LIMIT),
        name="pre_prompt",
    )(x, p['pre_mix_g'], p['wn_prompt'], p['gq'], p['gkv'],
      tabs['c32'], tabs['s32'], tabs['cd'], tabs['sd'], tabs['cm'], tabs['sm'],
      tabs['cmt'], tabs['smt'], tabs['cdt'], tabs['sdt'],
      p['wt'], p['wuqt'], p['wuqpt'], p['wuk_pad'], p['wuvt'])


def _pre_sample(x, p, tabs):
    rows = x.shape[0]
    tm = 256
    in_specs = [
        _row_spec(tm, D_MODEL), _full((1, D_MODEL)), _full(p['wn_sample'].shape),
        _full((1, MLA_Q_RANK)), _full((1, MLA_KV_RANK)),
        _row_spec(tm, 32), _row_spec(tm, 32), _row_spec(tm, 128), _row_spec(tm, 128),
        _row_spec(tm, 128), _row_spec(tm, 128),
        _full(p['wuq_pad'].shape), _full(p['wuqp_pad'].shape),
    ]
    out_specs = [
        _row_spec(tm, 256), _row_spec(tm, 32), _row_spec(tm, 512), _row_spec(tm, 512),
        _row_spec(tm, 1024), _row_spec(tm, 512), _row_spec(tm, 512),
    ]
    out_shape = [
        jax.ShapeDtypeStruct((rows, 256), F32), jax.ShapeDtypeStruct((rows, 32), F32),
        jax.ShapeDtypeStruct((rows, 512), F32), jax.ShapeDtypeStruct((rows, 512), F32),
        jax.ShapeDtypeStruct((rows, 1024), F32), jax.ShapeDtypeStruct((rows, 512), F32),
        jax.ShapeDtypeStruct((rows, 512), F32),
    ]
    return pl.pallas_call(
        _pre_sample_kernel,
        grid=(rows // tm,),
        in_specs=in_specs, out_specs=out_specs, out_shape=out_shape,
        compiler_params=pltpu.CompilerParams(dimension_semantics=("parallel",),
                                             vmem_limit_bytes=VMEM_LIMIT),
        name="pre_sample",
    )(x, p['pre_mix_g'], p['wn_sample'], p['gq'], p['gkv'],
      tabs['c32'], tabs['s32'], tabs['cd'], tabs['sd'], tabs['cm'], tabs['sm'],
      p['wuq_pad'], p['wuqp_pad'])


def _chunk_mask(i, j, reps):
    kc = (j * TK + lax.broadcasted_iota(jnp.int32, (TK, TQ), 0)) // CHUNK
    qc = (i * TQ + lax.broadcasted_iota(jnp.int32, (TK, TQ), 1)) // CHUNK
    mask = kc <= qc
    return mask if reps == 1 else jnp.concatenate([mask] * reps, axis=1)


def _flash_blocks(i, qk, process):
    qk(0, 0)

    def body(jj, c):
        j = 2 * jj
        qk(j + 1, 1)
        process(j, 0, False)
        qk(j + 2, 0)
        process(j + 1, 1, False)
        return c

    lax.fori_loop(0, i, body, 0)
    qk(2 * i + 1, 1)
    process(2 * i, 0, True)
    process(2 * i + 1, 1, True)


def _online_update(st, j, h, m_ref, acc_ref, vt_ref):
    m_old = m_ref[h]
    m_new = jnp.maximum(m_old, jnp.max(st, axis=0, keepdims=True))
    m_ref[h] = m_new
    p = jnp.exp2((st - m_new).astype(BF16))
    acc_ref[h] = jnp.exp2(m_old - m_new) * acc_ref[h] + _dot(vt_ref[h, j], p)


def _kblock(j):
    return pl.ds(pl.multiple_of(j * TK, TK), TK)


def _attn_mla_kernel(qt_ref, k_ref, vt_ref, o_ref, s_ref, m_ref, acc_ref):
    i = pl.program_id(1)
    heads = qt_ref.shape[0] // HEAD_PAD
    m_ref[...] = jnp.full(m_ref.shape, NEG_INF, F32)
    acc_ref[...] = jnp.zeros(acc_ref.shape, F32)

    def qk(j, slot):
        for h in range(heads):
            hs = slice(h * HEAD_PAD, (h + 1) * HEAD_PAD)
            s_ref[slot, h] = _dot(k_ref[_kblock(j), hs], qt_ref[hs, :])

    def process(j, slot, masked):
        for h in range(heads):
            st = s_ref[slot, h]
            if masked:
                st = jnp.where(_chunk_mask(i, j, 1), st, NEG_INF)
            _online_update(st, j, h, m_ref, acc_ref, vt_ref)

    _flash_blocks(i, qk, process)
    for h in range(heads):
        acc = acc_ref[h]
        o_ref[h * MLA_V:(h + 1) * MLA_V, :] = (
            acc[0:MLA_V, :] / acc[MLA_V:MLA_V + 1, :]).astype(BF16)


def _attn_mla(qmt, km, vmt):
    t = km.shape[0]
    hg = ATTN_HEADS_PER_STEP
    groups = MLA_HEADS // hg
    nq = t // TQ
    resident = pl.Buffered(1)
    return pl.pallas_call(
        _attn_mla_kernel,
        grid=(groups, nq),
        in_specs=[
            pl.BlockSpec((hg * HEAD_PAD, TQ), lambda g, i: (g, i)),
            pl.BlockSpec((t, hg * HEAD_PAD), lambda g, i: (0, g), pipeline_mode=resident),
            pl.BlockSpec((hg, t // TK, V_ROWS, TK), lambda g, i: (g, 0, 0, 0),
                         pipeline_mode=resident),
        ],
        out_specs=pl.BlockSpec((hg * MLA_V, TQ), lambda g, i: (g, i)),
        out_shape=jax.ShapeDtypeStruct((MLA_HEADS * MLA_V, t), BF16),
        scratch_shapes=[pltpu.VMEM((2, hg, TK, TQ), F32), pltpu.VMEM((hg, 1, TQ), F32),
                        pltpu.VMEM((hg, V_ROWS, TQ), F32)],
        compiler_params=pltpu.CompilerParams(dimension_semantics=("parallel", "parallel"),
                                             vmem_limit_bytes=VMEM_LIMIT),
        name="attn_mla",
    )(qmt, km, vmt)


def _attn_diff_kernel(lam_init, qt_ref, k_ref, vt_ref, lq1_ref, lk1_ref, lq2_ref, lk2_ref,
                      gsub_ref, o_ref, wq_ref, s_ref, m_ref, acc_ref):
    i = pl.program_id(1)
    heads = vt_ref.shape[0]
    grp = lax.broadcasted_iota(jnp.int32, (128, TQ), 0) // DIFF_DC
    for pr in range(heads // 2):
        qt = qt_ref[pr * 128:(pr + 1) * 128, :]
        zero = jnp.zeros_like(qt)
        for s in range(4):
            wq_ref[pr, :, s * TQ:(s + 1) * TQ] = jnp.where(grp == s, qt, zero)
    m_ref[...] = jnp.full(m_ref.shape, NEG_INF, F32)
    acc_ref[...] = jnp.zeros(acc_ref.shape, F32)

    def qk(j, slot):
        for h in range(heads):
            pr, hh = divmod(h, 2)
            k = k_ref[_kblock(j), pr * 128:(pr + 1) * 128]
            s_ref[slot, h] = _dot(k, wq_ref[pr, :, hh * 2 * TQ:(hh + 1) * 2 * TQ])

    def process(j, slot, masked):
        for h in range(heads):
            st = s_ref[slot, h]
            if masked:
                st = jnp.where(_chunk_mask(i, j, 2), st, NEG_INF)
            _online_update(st, j, h, m_ref, acc_ref, vt_ref)

    _flash_blocks(i, qk, process)

    lam = (jnp.exp(jnp.sum(lq1_ref[...] * lk1_ref[...], axis=-1, keepdims=True))
           - jnp.exp(jnp.sum(lq2_ref[...] * lk2_ref[...], axis=-1, keepdims=True))
           + lam_init)
    g = gsub_ref[...] * (1.0 - lam_init)
    for h in range(heads):
        acc = acc_ref[h]
        o0 = acc[0:DIFF_V, 0:TQ] / acc[DIFF_V:DIFF_V + 1, 0:TQ]
        o1 = acc[0:DIFF_V, TQ:2 * TQ] / acc[DIFF_V:DIFF_V + 1, TQ:2 * TQ]
        o = o0 - lam * o1
        o = o * lax.rsqrt(jnp.mean(o * o, axis=0, keepdims=True) + EPS) * g
        o_ref[h * DIFF_V:(h + 1) * DIFF_V, :] = o.astype(BF16)


def _attn_diff(qdt, kd, vdt, lq1, lk1, lq2, lk2, gsub, lam_init):
    t = kd.shape[0]
    nq = t // TQ
    hg = ATTN_HEADS_PER_STEP
    groups = DIFF_HEADS // hg
    vec = _full((1, DIFF_DC))
    resident = pl.Buffered(1)
    return pl.pallas_call(
        functools.partial(_attn_diff_kernel, lam_init),
        grid=(groups, nq),
        in_specs=[
            pl.BlockSpec((hg * DIFF_V, TQ), lambda g, i: (g, i)),
            pl.BlockSpec((t, hg * DIFF_V), lambda g, i: (0, g), pipeline_mode=resident),
            pl.BlockSpec((hg, t // TK, V_ROWS, TK), lambda g, i: (g, 0, 0, 0),
                         pipeline_mode=resident),
            vec, vec, vec, vec, _full((DIFF_V, 1)),
        ],
        out_specs=pl.BlockSpec((hg * DIFF_V, TQ), lambda g, i: (g, i)),
        out_shape=jax.ShapeDtypeStruct((DIFF_HEADS * DIFF_V, t), BF16),
        scratch_shapes=[pltpu.VMEM((hg // 2, 128, 4 * TQ), BF16),
                        pltpu.VMEM((2, hg, TK, 2 * TQ), F32),
                        pltpu.VMEM((hg, 1, 2 * TQ), F32),
                        pltpu.VMEM((hg, V_ROWS, 2 * TQ), F32)],
        compiler_params=pltpu.CompilerParams(dimension_semantics=("parallel", "parallel"),
                                             vmem_limit_bytes=VMEM_LIMIT),
        name="attn_diff",
    )(qdt, kd, vdt, lq1, lk1, lq2, lk2, gsub)


def _attn_mem_kernel(qt_ref, k_ref, vt_ref, o_ref):
    for h in range(MEM_HEADS):
        sl = slice(h * MEM_DH, (h + 1) * MEM_DH)
        st = _dot(k_ref[:, sl], qt_ref[sl, :])
        e = jnp.exp(st - jnp.max(st, axis=0, keepdims=True))
        p = (e / jnp.sum(e, axis=0, keepdims=True)).astype(BF16)
        o_ref[sl, :] = _dot(vt_ref[sl, :], p).astype(BF16)


def _attn_mem(qct, mk, mvt):
    w, t = qct.shape
    tq = 512
    return pl.pallas_call(
        _attn_mem_kernel,
        grid=(t // tq,),
        in_specs=[pl.BlockSpec((w, tq), lambda i: (0, i)), _full(mk.shape), _full(mvt.shape)],
        out_specs=pl.BlockSpec((w, tq), lambda i: (0, i)),
        out_shape=jax.ShapeDtypeStruct((w, t), BF16),
        compiler_params=pltpu.CompilerParams(dimension_semantics=("parallel",)),
        name="attn_mem",
    )(qct, mk, mvt)


def _softmax_two(s_a, s_b):
    m = jnp.maximum(jnp.max(s_a, axis=-1, keepdims=True), jnp.max(s_b, axis=-1, keepdims=True))
    e_a = jnp.exp(s_a - m)
    e_b = jnp.exp(s_b - m)
    inv = 1.0 / (jnp.sum(e_a, axis=-1, keepdims=True) + jnp.sum(e_b, axis=-1, keepdims=True))
    return e_a * inv, e_b * inv


def _diag_blocks(o_all, rows, width, nblk):
    lane_blk = lax.broadcasted_iota(jnp.int32, (rows, nblk * width), 1) // width
    out = jnp.zeros((rows, nblk * width), F32)
    for b in range(nblk):
        out = out + jnp.where(lane_blk == b, o_all[b * rows:(b + 1) * rows, :], 0.0)
    return out


def _sample_mla_kernel(qm_ref, ckvn_ref, krn_ref, mq_ref, ckvp_ref, krp_ref, mk_ref, mv_ref,
                       wabs_ref, wuv_ref, oa_ref, oc_ref):
    nq = qm_ref.shape[0]
    qm = qm_ref[...].astype(BF16)
    qext = jnp.concatenate(
        [_dot(qm[:, h * HEAD_PAD:(h + 1) * HEAD_PAD], wabs_ref[h]) for h in range(MLA_HEADS)],
        axis=0).astype(BF16)
    q_lat = qext[:, 0:MLA_KV_RANK]
    q_rope = qext[:, MLA_KV_RANK:MLA_KV_RANK + MLA_ROPE]
    ckv_p = ckvp_ref[...].astype(BF16)
    ckv_n = ckvn_ref[...].astype(BF16)
    s_p = _dot_nt(q_lat, ckv_p) + _dot_nt(q_rope, krp_ref[...].astype(BF16))
    s_n = _dot_nt(q_lat, ckv_n) + _dot_nt(q_rope, krn_ref[...].astype(BF16))
    p_p, p_n = _softmax_two(s_p, s_n)
    o_lat = _dot(p_p.astype(BF16), ckv_p) + _dot(p_n.astype(BF16), ckv_n)
    o_all = _dot(o_lat.astype(BF16), wuv_ref[...])
    oa_ref[...] = _diag_blocks(o_all, nq, MLA_V, MLA_HEADS)
    mq = mq_ref[...].astype(BF16)
    for h in range(MEM_HEADS):
        sl = slice(h * MEM_DH, (h + 1) * MEM_DH)
        s = _dot_nt(mq[:, sl], mk_ref[:, sl].astype(BF16))
        e = jnp.exp(s - jnp.max(s, axis=-1, keepdims=True))
        p = (e / jnp.sum(e, axis=-1, keepdims=True)).astype(BF16)
        oc_ref[:, sl] = _dot(p, mv_ref[:, sl].astype(BF16))


def _sample_mla(qm, ckv_new, kr_new, mq, ckv_past, kr_past, mem_k, mem_v, wabs, wuv, nb, nq):
    def rows(w):
        return pl.BlockSpec((nq, w), lambda b: (b, 0))

    def cache(shape):
        return pl.BlockSpec((None,) + shape, lambda b: (b, 0, 0))

    past = ckv_past.shape[1]
    return pl.pallas_call(
        _sample_mla_kernel,
        grid=(nb,),
        in_specs=[rows(1024), rows(256), rows(32), rows(512),
                  cache((past, 256)), cache((past, 32)), cache(mem_k.shape[1:]),
                  cache(mem_v.shape[1:]), _full(wabs.shape), _full(wuv.shape)],
        out_specs=[rows(512), rows(512)],
        out_shape=[jax.ShapeDtypeStruct((nb * nq, 512), F32)] * 2,
        compiler_params=pltpu.CompilerParams(dimension_semantics=("parallel",),
                                             vmem_limit_bytes=VMEM_LIMIT),
        name="sample_mla_mem",
    )(qm, ckv_new, kr_new, mq, ckv_past, kr_past, mem_k, mem_v, wabs, wuv)


def _sample_diff_kernel(lam_init, dq_ref, dkn_ref, dvn_ref, dkp_ref, dvp_ref,
                        lq1_ref, lk1_ref, lq2_ref, lk2_ref, o_ref):
    nq = dq_ref.shape[0]
    w = dq_ref.shape[1]
    nmaps = w // DIFF_DC
    q = dq_ref[...]
    row_grp = lax.broadcasted_iota(jnp.int32, (nmaps * nq, w), 0) // nq
    lane_grp = lax.broadcasted_iota(jnp.int32, (nmaps * nq, w), 1) // DIFF_DC
    qbd = jnp.where(row_grp == lane_grp, jnp.concatenate([q] * nmaps, axis=0), 0.0).astype(BF16)
    k_p = dkp_ref[...].astype(BF16)
    k_n = dkn_ref[...].astype(BF16)
    p_p, p_n = _softmax_two(_dot_nt(qbd, k_p), _dot_nt(qbd, k_n))
    lam = (jnp.exp(jnp.sum(lq1_ref[...] * lk1_ref[...], axis=-1, keepdims=True))
           - jnp.exp(jnp.sum(lq2_ref[...] * lk2_ref[...], axis=-1, keepdims=True))
           + lam_init)

    def combine(p):
        parts = [p[(2 * h) * nq:(2 * h + 1) * nq, :] - lam * p[(2 * h + 1) * nq:(2 * h + 2) * nq, :]
                 for h in range(nmaps // 2)]
        return jnp.concatenate(parts, axis=0).astype(BF16)

    o_all = (_dot(combine(p_p), dvp_ref[...].astype(BF16))
             + _dot(combine(p_n), dvn_ref[...].astype(BF16)))
    o_ref[...] = _diag_blocks(o_all, nq, DIFF_V, nmaps // 2)


def _sample_diff(dq, dk_new, dv_new, dk_past, dv_past, lq1, lk1, lq2, lk2, lam_init, nb, nq):
    hw = 256
    past = dk_past.shape[1]
    rows = pl.BlockSpec((nq, hw), lambda b, g: (b, g))
    cache = pl.BlockSpec((None, past, hw), lambda b, g: (b, 0, g))
    vec = _full((1, DIFF_DC))
    return pl.pallas_call(
        functools.partial(_sample_diff_kernel, lam_init),
        grid=(nb, 512 // hw),
        in_specs=[rows, rows, rows, cache, cache, vec, vec, vec, vec],
        out_specs=rows,
        out_shape=jax.ShapeDtypeStruct((nb * nq, 512), F32),
        compiler_params=pltpu.CompilerParams(dimension_semantics=("parallel", "parallel"),
                                             vmem_limit_bytes=VMEM_LIMIT),
        name="sample_diff",
    )(dq, dk_new, dv_new, dk_past, dv_past, lq1, lk1, lq2, lk2)


def _subln_kernel(lam_init, o_ref, g_ref, out_ref):
    g = g_ref[...] * (1.0 - lam_init)
    for h in range(DIFF_HEADS):
        o = o_ref[h * DIFF_V:(h + 1) * DIFF_V, :]
        out_ref[h * DIFF_V:(h + 1) * DIFF_V, :] = (
            o * lax.rsqrt(jnp.mean(o * o, axis=0, keepdims=True) + EPS) * g).astype(BF16)


def _subln(ot, gsub, lam_init):
    return pl.pallas_call(
        functools.partial(_subln_kernel, lam_init),
        out_shape=jax.ShapeDtypeStruct(ot.shape, BF16),
        name="sample_subln",
    )(ot, gsub)


def _mix_kernel(x_ref, oa_ref, ob_ref, oc_ref, g_ref, wg_ref, bg_ref, woa_ref, wob_ref, woc_ref,
                wout_ref, gpost_ref, y_ref):
    x = x_ref[...]
    xn = _rms(x, g_ref[...]).astype(BF16)
    d = x.shape[1]
    merged = None
    for b, (o_ref, w_ref) in enumerate(((oa_ref, woa_ref), (ob_ref, wob_ref), (oc_ref, woc_ref))):
        gate = jax.nn.sigmoid(_dot(xn, wg_ref[:, b * d:(b + 1) * d]) + bg_ref[:, b * d:(b + 1) * d])
        term = gate * _dot_tn(o_ref[...], w_ref[...])
        merged = term if merged is None else merged + term
    mix = _dot(merged.astype(BF16), wout_ref[...])
    y_ref[...] = x + _rms(mix, gpost_ref[...])


def _mix(x, oat, obt, oct, p):
    rows, d = x.shape
    tm = 512 if rows % 512 == 0 else rows
    ot_spec = pl.BlockSpec((512, tm), lambda i: (0, i))
    return pl.pallas_call(
        _mix_kernel,
        grid=(rows // tm,),
        in_specs=[_row_spec(tm, d), ot_spec, ot_spec, ot_spec, _full((1, d)),
                  _full(p['w_gate'].shape), _full((1, 3 * d)), _full((512, d)), _full((512, d)),
                  _full((512, d)), _full((d, d)), _full((1, d))],
        out_specs=_row_spec(tm, d),
        out_shape=jax.ShapeDtypeStruct((rows, d), F32),
        compiler_params=pltpu.CompilerParams(dimension_semantics=("parallel",),
                                             vmem_limit_bytes=VMEM_LIMIT),
        name="mix",
    )(x, oat, obt, oct, p['pre_mix_g'], p['w_gate'], p['b_gate'], p['w_o_mla'], p['w_o_diff'],
      p['w_o_mem'], p['w_out'], p['post_mix_g'])


def _mlp_kernel(x_ref, g_ref, wup_ref, wdn_ref, gpost_ref, y_ref):
    x = x_ref[...]
    h = _rms(x, g_ref[...]).astype(BF16)
    u = jnp.maximum(_dot(h, wup_ref[...]), 0.0)
    f = _dot((u * u).astype(BF16), wdn_ref[...])
    y_ref[...] = x + _rms(f, gpost_ref[...])


def _mlp(x, p):
    rows, d = x.shape
    tm = 256
    return pl.pallas_call(
        _mlp_kernel,
        grid=(rows // tm,),
        in_specs=[_row_spec(tm, d), _full((1, d)), _full(p['w_mlp_up'].shape),
                  _full(p['w_mlp_down'].shape), _full((1, d))],
        out_specs=_row_spec(tm, d),
        out_shape=jax.ShapeDtypeStruct((rows, d), F32),
        compiler_params=pltpu.CompilerParams(dimension_semantics=("parallel",),
                                             vmem_limit_bytes=VMEM_LIMIT),
        name="mlp",
    )(x, p['pre_mlp_g'], p['w_mlp_up'], p['w_mlp_down'], p['post_mlp_g'])


def _partner(width, group, half):
    idx = np.arange(width)
    sign = np.zeros(width, np.float32)
    d = idx % group
    first = d < half
    second = (d >= half) & (d < 2 * half)
    src = np.where(first, idx + half, np.where(second, idx - half, idx))
    sign[first] = -1.0
    sign[second] = 1.0
    return src, sign


def _take_signed(w, src, sign):
    return w[:, src] * jnp.asarray(sign)[None, :]


def _pad_cols(w, total):
    return jnp.pad(w, ((0, 0), (0, total - w.shape[1])))


def _prep_layer(l, w_in, mla_w_uq, mla_w_uk, mla_w_uv, w_mem_k, w_mem_v, w_o_mla, w_o_diff, w_o_mem,
                w_gate, b_gate, w_out, w_mlp_up, w_mlp_down, gains):
    cq, ckv, kr, dq, dk, dv, mq = jnp.split(
        w_in[l], np.cumsum((384, 256, 32, 512, 512, 512))[...].tolist(), axis=1)
    src_d, sign_d = _partner(512, DIFF_DC, DIFF_ROT // 2)
    src_r, sign_r = _partner(32, 32, MLA_ROPE // 2)
    dkp = _take_signed(dk, src_d, sign_d)
    dqp = _take_signed(dq, src_d, sign_d)
    krp = _take_signed(kr, src_r, sign_r)
    common = [cq, ckv, dk, dkp, dv, _pad_cols(kr, 128), _pad_cols(krp, 128)]
    zeros64 = jnp.zeros((D_MODEL, MLA_NOPE), F32)
    kr128 = _pad_cols(jnp.concatenate([zeros64, kr], axis=1), 128)
    kr128p = _pad_cols(jnp.concatenate([zeros64, krp], axis=1), 128)
    head_w = MLA_NOPE + MLA_ROPE
    e = np.arange(HEAD_PAD)
    valid = e < head_w
    src_q = np.concatenate([h * head_w + np.where(valid, e, 0) for h in range(MLA_HEADS)])
    sign_q = np.tile(valid.astype(np.float32), MLA_HEADS)
    in_rope1 = (e >= MLA_NOPE) & (e < MLA_NOPE + MLA_ROPE // 2)
    in_rope2 = (e >= MLA_NOPE + MLA_ROPE // 2) & valid
    pe = np.where(in_rope1, e + MLA_ROPE // 2, np.where(in_rope2, e - MLA_ROPE // 2, 0))
    src_qp = np.concatenate([h * head_w + pe for h in range(MLA_HEADS)])
    sign_qp = np.tile(np.where(in_rope1, -1.0, np.where(in_rope2, 1.0, 0.0)).astype(np.float32),
                      MLA_HEADS)
    wuq_pad = _take_signed(mla_w_uq[l], src_q, sign_q)
    wuqp_pad = _take_signed(mla_w_uq[l], src_qp, sign_qp)
    wuk = mla_w_uk[l]
    wuk_pad = jnp.pad(wuk, ((0, 0), (0, 0), (0, HEAD_PAD - MLA_NOPE))).reshape(MLA_KV_RANK, -1)
    wuv = mla_w_uv[l].reshape(MLA_KV_RANK, MLA_HEADS * MLA_V)
    sel = np.zeros((HEAD_PAD, 128), np.float32)
    sel[MLA_NOPE + np.arange(MLA_ROPE), np.arange(MLA_ROPE)] = 1.0
    wabs = jnp.concatenate([
        jnp.pad(jnp.transpose(wuk, (1, 2, 0)), ((0, 0), (0, HEAD_PAD - MLA_NOPE), (0, 0))),
        jnp.broadcast_to(jnp.asarray(sel), (MLA_HEADS, HEAD_PAD, 128))], axis=2)
    bf = lambda a: a.astype(BF16)
    row = lambda a: a[l][None, :]
    p = {
        'wn_prompt': bf(jnp.concatenate(common + [kr128, kr128p], axis=1)),
        'wn_sample': bf(jnp.concatenate(common + [dq, dqp, mq], axis=1)),
        'wt': bf(jnp.concatenate([dq, dqp, mq, dv], axis=1).T),
        'wuqt': bf(wuq_pad.T), 'wuqpt': bf(wuqp_pad.T),
        'wuq_pad': bf(wuq_pad), 'wuqp_pad': bf(wuqp_pad),
        'wuk_pad': bf(wuk_pad), 'wuvt': bf(wuv.T), 'wuv': bf(wuv), 'wabs': bf(wabs),
        'w_mem_k': bf(w_mem_k[l]), 'w_mem_v': bf(w_mem_v[l]), 'w_mem_vt': bf(w_mem_v[l].T),
        'w_o_mla': bf(w_o_mla[l]), 'w_o_diff': bf(w_o_diff[l]), 'w_o_mem': bf(w_o_mem[l]),
        'w_gate': bf(w_gate[l]), 'b_gate': row(b_gate), 'w_out': bf(w_out[l]),
        'w_mlp_up': bf(w_mlp_up[l]), 'w_mlp_down': bf(w_mlp_down[l]),
    }
    for name, g in gains.items():
        p[name] = row(g)
    return p


def _rope_cos_sin(pos, rot_dim, theta):
    half = rot_dim // 2
    inv = jnp.power(jnp.float32(theta), -jnp.arange(half, dtype=F32) * (2.0 / rot_dim))
    ang = pos.astype(F32)[:, None] * inv[None, :]
    return jnp.cos(ang), jnp.sin(ang)


def _tables(pos, reps):
    n = pos.shape[0]
    cm, sm = _rope_cos_sin(pos, MLA_ROPE, MLA_THETA)
    cd, sd = _rope_cos_sin(pos, DIFF_ROT, ROPE_THETA)
    one = lambda w: jnp.ones((n, w), F32)
    zero = lambda w: jnp.zeros((n, w), F32)
    t = {
        'c32': jnp.concatenate([cm, cm], axis=1), 's32': jnp.concatenate([sm, sm], axis=1),
        'cm': jnp.concatenate([one(64), cm, cm, one(32)], axis=1),
        'sm': jnp.concatenate([zero(64), sm, sm, zero(32)], axis=1),
        'cd': jnp.tile(jnp.concatenate([cd, cd, one(24)], axis=1), (1, 4)),
        'sd': jnp.tile(jnp.concatenate([sd, sd, zero(24)], axis=1), (1, 4)),
    }
    return {k: jnp.tile(v, (reps, 1)) for k, v in t.items()}


def kernel(x_prompt, x_sample, cache_mla_ckv, cache_mla_krope, cache_diff_k, cache_diff_v, cache_mem_k, cache_mem_v, mem_prompt, pre_mix_g, w_in, mla_q_norm_g, mla_w_uq, mla_kv_norm_g, mla_w_uk, mla_w_uv, diff_lq1, diff_lk1, diff_lq2, diff_lk2, diff_subln_g, mem_norm_g, w_mem_k, w_mem_v, w_o_mla, w_o_diff, w_o_mem, w_gate, b_gate, w_out, post_mix_g, pre_mlp_g, w_mlp_up, w_mlp_down, post_mlp_g):
    depth = w_in.shape[0]
    bp, t, d = x_prompt.shape
    nb, nq, _ = x_sample.shape
    past = cache_mla_ckv.shape[2]
    assert bp == 1 and t % TQ == 0 and TQ == 2 * TK and TK % CHUNK == 0 and d == D_MODEL
    assert past % CHUNK == 0 and nq <= CHUNK

    tabs_p = _tables(jnp.arange(t, dtype=jnp.int32), 1)
    tabs_p['cmt'] = (tabs_p['cm'] * (MLA_SCALE * LOG2E)).T
    tabs_p['smt'] = (tabs_p['sm'] * (MLA_SCALE * LOG2E)).T
    tabs_p['cdt'] = (tabs_p['cd'] * (DIFF_SCALE * LOG2E)).T
    tabs_p['sdt'] = (tabs_p['sd'] * (DIFF_SCALE * LOG2E)).T
    tabs_s = _tables(past + jnp.arange(nq, dtype=jnp.int32), nb)

    xp = x_prompt.reshape(t, d)
    xs = x_sample.reshape(nb * nq, d)
    outs = {k: [] for k in ('p_ckv', 'p_kr', 'p_dk', 'p_dv', 'p_mk', 'p_mv',
                            's_ckv', 's_kr', 's_dk', 's_dv')}
    for l in range(depth):
        lam_init = 0.8 - 0.6 * math.exp(-0.3 * l)
        p = _prep_layer(l, w_in, mla_w_uq, mla_w_uk, mla_w_uv, w_mem_k, w_mem_v, w_o_mla, w_o_diff,
                        w_o_mem, w_gate, b_gate, w_out, w_mlp_up, w_mlp_down,
                        {'pre_mix_g': pre_mix_g, 'gq': mla_q_norm_g, 'gkv': mla_kv_norm_g,
                         'mem_norm_g': mem_norm_g, 'post_mix_g': post_mix_g,
                         'pre_mlp_g': pre_mlp_g, 'post_mlp_g': post_mlp_g})
        lq1, lk1, lq2, lk2 = (a[l][None, :] for a in (diff_lq1, diff_lk1, diff_lq2, diff_lk2))
        gsub = diff_subln_g[l][:, None]

        mk, mv, mkb, mvt = _memkv(mem_prompt[0], p['mem_norm_g'], p['w_mem_k'], p['w_mem_v'],
                                  p['w_mem_vt'])
        (ckv_p, kr_p, dk_p, dv_p, kd, km, vmt, qmt, qdt, qct, vdt) = _pre_prompt(xp, p, tabs_p)
        oat = _attn_mla(qmt, km, vmt)
        obt = _attn_diff(qdt, kd, vdt, lq1, lk1, lq2, lk2, gsub, lam_init)
        oct = _attn_mem(qct, mkb, mvt)
        xp = _mlp(_mix(xp, oat, obt, oct, p), p)

        (ckv_s, kr_s, dk_s, dv_s, qm_s, dq_s, mq_s) = _pre_sample(xs, p, tabs_s)
        oa_s, oc_s = _sample_mla(qm_s, ckv_s, kr_s, mq_s, cache_mla_ckv[l], cache_mla_krope[l],
                                 cache_mem_k[l].reshape(nb, -1, 512),
                                 cache_mem_v[l].reshape(nb, -1, 512), p['wabs'], p['wuv'], nb, nq)
        ob_s = _sample_diff(dq_s, dk_s, dv_s, cache_diff_k[l].reshape(nb, past, 512),
                            cache_diff_v[l].reshape(nb, past, 512), lq1, lk1, lq2, lk2,
                            lam_init, nb, nq)
        obt_s = _subln(ob_s.T, gsub, lam_init)
        xs = _mlp(_mix(xs, oa_s.T.astype(BF16), obt_s, oc_s.T.astype(BF16), p), p)

        outs['p_ckv'].append(ckv_p.reshape(1, t, MLA_KV_RANK))
        outs['p_kr'].append(kr_p.reshape(1, t, MLA_ROPE))
        outs['p_dk'].append(dk_p.reshape(1, t, DIFF_HEADS, DIFF_V))
        outs['p_dv'].append(dv_p.reshape(1, t, DIFF_HEADS, DIFF_V))
        outs['p_mk'].append(mk.reshape(1, -1, MEM_HEADS, MEM_DH))
        outs['p_mv'].append(mv.reshape(1, -1, MEM_HEADS, MEM_DH))
        outs['s_ckv'].append(ckv_s.reshape(nb, nq, MLA_KV_RANK))
        outs['s_kr'].append(kr_s.reshape(nb, nq, MLA_ROPE))
        outs['s_dk'].append(dk_s.reshape(nb, nq, DIFF_HEADS, DIFF_V))
        outs['s_dv'].append(dv_s.reshape(nb, nq, DIFF_HEADS, DIFF_V))

    st = lambda k: jnp.stack(outs[k], axis=0)
    return (xp.reshape(1, t, d), xs.reshape(nb, nq, d),
            st('p_ckv'), st('p_kr'), st('p_dk'), st('p_dv'), st('p_mk'), st('p_mv'),
            st('s_ckv'), st('s_kr'), st('s_dk'), st('s_dv'))
```

```python
import functools
import math

import numpy as np
import jax
import jax.numpy as jnp
from jax import lax
from jax.experimental import pallas as pl
from jax.experimental.pallas import tpu as pltpu

F32 = jnp.float32
BF16 = jnp.bfloat16

D_MODEL = 1024
CHUNK = 64
EPS = 1e-6
NEG_INF = -1e30
MLA_HEADS = 8
MLA_Q_RANK = 384
MLA_KV_RANK = 256
MLA_NOPE = 64
MLA_ROPE = 32
MLA_V = 64
MLA_THETA = 10000.0
MLA_SCALE = (MLA_NOPE + MLA_ROPE) ** -0.5
DIFF_HEADS = 8
DIFF_DC = 32
DIFF_V = 64
DIFF_ROT = 8
ROPE_THETA = 500000.0
DIFF_SCALE = DIFF_DC ** -0.5
MEM_HEADS = 4
MEM_DH = 128
MEM_SCALE = MEM_DH ** -0.5
LOG2E = math.log2(math.e)
HEAD_PAD = 128
V_ROWS = 80
TQ = 512
TK = 256
ATTN_HEADS_PER_STEP = 4
MLA_PV_LAG = 1
DIFF_PV_LAG = 0
VMEM_LIMIT = 56 * 1024 * 1024

_NT = (((1,), (1,)), ((), ()))
_TN = (((0,), (0,)), ((), ()))


def _dot(a, b):
    return jnp.dot(a, b, preferred_element_type=F32)


def _dot_nt(a, b):
    return lax.dot_general(a, b, _NT, preferred_element_type=F32)


def _dot_tn(a, b):
    return lax.dot_general(a, b, _TN, preferred_element_type=F32)


def _rms(x, g):
    return x * lax.rsqrt(jnp.mean(x * x, axis=-1, keepdims=True) + EPS) * g


def _full(shape):
    nd = len(shape)
    return pl.BlockSpec(shape, lambda *_: (0,) * nd)


def _memkv_kernel(mem_ref, g_ref, wk_ref, wv_ref, wvt_ref, k_ref, v_ref, kb_ref, vt_ref):
    mn = _rms(mem_ref[...], g_ref[...]).astype(BF16)
    k = _dot(mn, wk_ref[...])
    k_ref[...] = k
    kb_ref[...] = k.astype(BF16)
    v_ref[...] = _dot(mn, wv_ref[...])
    vt_ref[...] = _dot_nt(wvt_ref[...], mn).astype(BF16)


def _memkv(mem, g, wk, wv, wvt):
    n_mem = mem.shape[0]
    w = wk.shape[1]
    return pl.pallas_call(
        _memkv_kernel,
        out_shape=(jax.ShapeDtypeStruct((n_mem, w), F32), jax.ShapeDtypeStruct((n_mem, w), F32),
                   jax.ShapeDtypeStruct((n_mem, w), BF16), jax.ShapeDtypeStruct((w, n_mem), BF16)),
        name="memkv",
    )(mem, g, wk, wv, wvt)


_N_CQ, _N_CKV, _N_DK, _N_DKP, _N_DV, _N_KR, _N_KRP = 0, 384, 640, 1152, 1664, 2176, 2304
_N_COMMON = 2432
_N_KR128, _N_KR128P = 2432, 2560
_N_PROMPT = 2688
_N_DQ, _N_DQP, _N_MQ = 2432, 2944, 3456
_N_SAMPLE = 3968


def _pre_common(x_ref, g_ref, wn_ref, gq_ref, gkv_ref, c32_ref, s32_ref, cd_ref, sd_ref,
                ckv_o, kr_o, dk_o, dv_o):
    xn = _rms(x_ref[...], g_ref[...]).astype(BF16)
    proj = _dot(xn, wn_ref[...])
    cqn = _rms(proj[:, _N_CQ:_N_CQ + MLA_Q_RANK], gq_ref[...]).astype(BF16)
    ckvn = _rms(proj[:, _N_CKV:_N_CKV + MLA_KV_RANK], gkv_ref[...])
    ckv_o[...] = ckvn
    kr_o[...] = (proj[:, _N_KR:_N_KR + MLA_ROPE] * c32_ref[...]
                 + proj[:, _N_KRP:_N_KRP + MLA_ROPE] * s32_ref[...])
    cd = cd_ref[...]
    sd = sd_ref[...]
    dk_tiles = []
    for j in range(4):
        lo = j * 128
        dk_j = (proj[:, _N_DK + lo:_N_DK + lo + 128] * cd
                + proj[:, _N_DKP + lo:_N_DKP + lo + 128] * sd)
        dk_o[:, lo:lo + 128] = dk_j
        dk_tiles.append(dk_j)
    dv_o[...] = proj[:, _N_DV:_N_DV + 512]
    return xn, proj, cqn, ckvn.astype(BF16), dk_tiles


def _pre_prompt_kernel(x_ref, g_ref, wn_ref, gq_ref, gkv_ref, c32_ref, s32_ref, cd_ref, sd_ref,
                       cm_ref, sm_ref, cmt_ref, smt_ref, cdt_ref, sdt_ref,
                       wt_ref, wuqt_ref, wuqpt_ref, wuk_ref, wuvt_ref,
                       ckv_o, kr_o, dk_o, dv_o, kd_o, km_o, vmt_o, qmt_o, qdt_o, qct_o, vdt_o):
    xn, proj, cqn, ckvb, dk_tiles = _pre_common(
        x_ref, g_ref, wn_ref, gq_ref, gkv_ref, c32_ref, s32_ref, cd_ref, sd_ref,
        ckv_o, kr_o, dk_o, dv_o)
    tm = x_ref.shape[0]
    for j in range(4):
        kd_o[:, j * 128:(j + 1) * 128] = dk_tiles[j].astype(BF16)
    kr128 = (proj[:, _N_KR128:_N_KR128 + 128] * cm_ref[...]
             + proj[:, _N_KR128P:_N_KR128P + 128] * sm_ref[...])
    knope = _dot(ckvb, wuk_ref[...])
    for h in range(MLA_HEADS):
        lo = h * HEAD_PAD
        km_o[:, lo:lo + HEAD_PAD] = (knope[:, lo:lo + HEAD_PAD] + kr128).astype(BF16)
    row = lax.broadcasted_iota(jnp.int32, (V_ROWS - MLA_V, tm), 0)
    ones_rows = jnp.where(row == 0, 1.0, 0.0).astype(BF16)
    vmt = _dot_nt(wuvt_ref[...], ckvb)
    projt = _dot_nt(wt_ref[...], xn)
    for h in range(MLA_HEADS):
        vmt_o[h, 0, 0:MLA_V, :] = vmt[h * MLA_V:(h + 1) * MLA_V, :].astype(BF16)
        vmt_o[h, 0, MLA_V:V_ROWS, :] = ones_rows
        vdt_o[h, 0, 0:DIFF_V, :] = projt[1536 + h * DIFF_V:1536 + (h + 1) * DIFF_V, :].astype(BF16)
        vdt_o[h, 0, DIFF_V:V_ROWS, :] = ones_rows
    qmt = _dot_nt(wuqt_ref[...], cqn)
    qmpt = _dot_nt(wuqpt_ref[...], cqn)
    cmt = cmt_ref[...]
    smt = smt_ref[...]
    for h in range(MLA_HEADS):
        lo = h * HEAD_PAD
        qmt_o[lo:lo + HEAD_PAD, :] = (qmt[lo:lo + HEAD_PAD, :] * cmt
                                      + qmpt[lo:lo + HEAD_PAD, :] * smt).astype(BF16)
    cdt = cdt_ref[...]
    sdt = sdt_ref[...]
    for j in range(4):
        lo = j * 128
        qdt_o[lo:lo + 128, :] = (projt[lo:lo + 128, :] * cdt
                                 + projt[512 + lo:512 + lo + 128, :] * sdt).astype(BF16)
    qct_o[...] = (projt[1024:1536, :] * MEM_SCALE).astype(BF16)


def _pre_sample_kernel(x_ref, g_ref, wn_ref, gq_ref, gkv_ref, c32_ref, s32_ref, cd_ref, sd_ref,
                       cm_ref, sm_ref, wuq_ref, wuqp_ref,
                       ckv_o, kr_o, dk_o, dv_o, qm_o, dq_o, mq_o):
    xn, proj, cqn, ckvb, dk_tiles = _pre_common(
        x_ref, g_ref, wn_ref, gq_ref, gkv_ref, c32_ref, s32_ref, cd_ref, sd_ref,
        ckv_o, kr_o, dk_o, dv_o)
    qm = _dot(cqn, wuq_ref[...])
    qmp = _dot(cqn, wuqp_ref[...])
    cm = cm_ref[...] * MLA_SCALE
    sm = sm_ref[...] * MLA_SCALE
    for h in range(MLA_HEADS):
        lo = h * HEAD_PAD
        qm_o[:, lo:lo + HEAD_PAD] = qm[:, lo:lo + HEAD_PAD] * cm + qmp[:, lo:lo + HEAD_PAD] * sm
    cd = cd_ref[...] * DIFF_SCALE
    sd = sd_ref[...] * DIFF_SCALE
    for j in range(4):
        lo = j * 128
        dq_o[:, lo:lo + 128] = (proj[:, _N_DQ + lo:_N_DQ + lo + 128] * cd
                                + proj[:, _N_DQP + lo:_N_DQP + lo + 128] * sd)
    mq_o[...] = proj[:, _N_MQ:_N_MQ + 512] * MEM_SCALE


def _row_spec(tm, w):
    return pl.BlockSpec((tm, w), lambda i: (i, 0))


def _col_spec(h, tm):
    return pl.BlockSpec((h, tm), lambda i: (0, i))


def _pre_prompt(x, p, tabs):
    rows = x.shape[0]
    tm = TK
    nkb = rows // tm
    in_specs = [
        _row_spec(tm, D_MODEL), _full((1, D_MODEL)), _full(p['wn_prompt'].shape),
        _full((1, MLA_Q_RANK)), _full((1, MLA_KV_RANK)),
        _row_spec(tm, 32), _row_spec(tm, 32), _row_spec(tm, 128), _row_spec(tm, 128),
        _row_spec(tm, 128), _row_spec(tm, 128),
        _col_spec(128, tm), _col_spec(128, tm), _col_spec(128, tm), _col_spec(128, tm),
        _full(p['wt'].shape), _full(p['wuqt'].shape), _full(p['wuqpt'].shape),
        _full(p['wuk_pad'].shape), _full(p['wuvt'].shape),
    ]
    vt_spec = pl.BlockSpec((8, 1, V_ROWS, tm), lambda i: (0, i, 0, 0))
    out_specs = [
        _row_spec(tm, 256), _row_spec(tm, 32), _row_spec(tm, 512), _row_spec(tm, 512),
        _row_spec(tm, 512), _row_spec(tm, 1024), vt_spec,
        _col_spec(1024, tm), _col_spec(512, tm), _col_spec(512, tm), vt_spec,
    ]
    out_shape = [
        jax.ShapeDtypeStruct((rows, 256), F32), jax.ShapeDtypeStruct((rows, 32), F32),
        jax.ShapeDtypeStruct((rows, 512), F32), jax.ShapeDtypeStruct((rows, 512), F32),
        jax.ShapeDtypeStruct((rows, 512), BF16), jax.ShapeDtypeStruct((rows, 1024), BF16),
        jax.ShapeDtypeStruct((8, nkb, V_ROWS, tm), BF16),
        jax.ShapeDtypeStruct((1024, rows), BF16), jax.ShapeDtypeStruct((512, rows), BF16),
        jax.ShapeDtypeStruct((512, rows), BF16),
        jax.ShapeDtypeStruct((8, nkb, V_ROWS, tm), BF16),
    ]
    return pl.pallas_call(
        _pre_prompt_kernel,
        grid=(rows // tm,),
        in_specs=in_specs, out_specs=out_specs, out_shape=out_shape,
        compiler_params=pltpu.CompilerParams(dimension_semantics=("parallel",),
                                             vmem_limit_bytes=VMEM_LIMIT),
        name="pre_prompt",
    )(x, p['pre_mix_g'], p['wn_prompt'], p['gq'], p['gkv'],
      tabs['c32'], tabs['s32'], tabs['cd'], tabs['sd'], tabs['cm'], tabs['sm'],
      tabs['cmt'], tabs['smt'], tabs['cdt'], tabs['sdt'],
      p['wt'], p['wuqt'], p['wuqpt'], p['wuk_pad'], p['wuvt'])


def _pre_sample(x, p, tabs):
    rows = x.shape[0]
    tm = 256
    in_specs = [
        _row_spec(tm, D_MODEL), _full((1, D_MODEL)), _full(p['wn_sample'].shape),
        _full((1, MLA_Q_RANK)), _full((1, MLA_KV_RANK)),
        _row_spec(tm, 32), _row_spec(tm, 32), _row_spec(tm, 128), _row_spec(tm, 128),
        _row_spec(tm, 128), _row_spec(tm, 128),
        _full(p['wuq_pad'].shape), _full(p['wuqp_pad'].shape),
    ]
    out_specs = [
        _row_spec(tm, 256), _row_spec(tm, 32), _row_spec(tm, 512), _row_spec(tm, 512),
        _row_spec(tm, 1024), _row_spec(tm, 512), _row_spec(tm, 512),
    ]
    out_shape = [
        jax.ShapeDtypeStruct((rows, 256), F32), jax.ShapeDtypeStruct((rows, 32), F32),
        jax.ShapeDtypeStruct((rows, 512), F32), jax.ShapeDtypeStruct((rows, 512), F32),
        jax.ShapeDtypeStruct((rows, 1024), F32), jax.ShapeDtypeStruct((rows, 512), F32),
        jax.ShapeDtypeStruct((rows, 512), F32),
    ]
    return pl.pallas_call(
        _pre_sample_kernel,
        grid=(rows // tm,),
        in_specs=in_specs, out_specs=out_specs, out_shape=out_shape,
        compiler_params=pltpu.CompilerParams(dimension_semantics=("parallel",),
                                             vmem_limit_bytes=VMEM_LIMIT),
        name="pre_sample",
    )(x, p['pre_mix_g'], p['wn_sample'], p['gq'], p['gkv'],
      tabs['c32'], tabs['s32'], tabs['cd'], tabs['sd'], tabs['cm'], tabs['sm'],
      p['wuq_pad'], p['wuqp_pad'])


def _chunk_mask(i, j, reps):
    kc = (j * TK + lax.broadcasted_iota(jnp.int32, (TK, TQ), 0)) // CHUNK
    qc = (i * TQ + lax.broadcasted_iota(jnp.int32, (TK, TQ), 1)) // CHUNK
    mask = kc <= qc
    return mask if reps == 1 else jnp.concatenate([mask] * reps, axis=1)


def _flash_blocks(i, heads, pv_lag, qk, softmax, pv):
    def run(steps, masked):
        pending = []
        for j_next, slot_next, j, slot in steps:
            for h in range(heads):
                if j_next is not None:
                    qk(j_next, slot_next, h)
                pending.append((j, h) + softmax(j, slot, h, masked))
                if len(pending) > pv_lag:
                    pv(*pending.pop(0))
        for item in pending:
            pv(*item)

    for h in range(heads):
        qk(0, 0, h)

    def pair(j):
        return [(j + 1, 1, j, 0), (j + 2, 0, j + 1, 1)]

    def body4(jj, c):
        run(pair(4 * jj) + pair(4 * jj + 2), False)
        return c

    def body2(jj, c):
        run(pair(2 * (i - 1)), False)
        return c

    lax.fori_loop(0, i // 2, body4, 0)
    lax.fori_loop(0, i % 2, body2, 0)
    run([(2 * i + 1, 1, 2 * i, 0), (None, None, 2 * i + 1, 1)], True)


def _online_softmax(st, h, m_ref):
    m_old = m_ref[h]
    m_new = jnp.maximum(m_old, jnp.max(st, axis=0, keepdims=True))
    m_ref[h] = m_new
    return jnp.exp2(m_old - m_new), jnp.exp2((st - m_new).astype(BF16))


def _online_pv(j, h, alpha, p, acc_ref, vt_ref):
    acc_ref[h] = alpha * acc_ref[h] + _dot(vt_ref[h, j], p)


def _kblock(j):
    return pl.ds(pl.multiple_of(j * TK, TK), TK)


def _attn_mla_kernel(qt_ref, k_ref, vt_ref, o_ref, s_ref, m_ref, acc_ref):
    i = pl.program_id(1)
    heads = qt_ref.shape[0] // HEAD_PAD
    m_ref[...] = jnp.full(m_ref.shape, NEG_INF, F32)
    acc_ref[...] = jnp.zeros(acc_ref.shape, F32)

    def qk(j, slot, h):
        hs = slice(h * HEAD_PAD, (h + 1) * HEAD_PAD)
        s_ref[slot, h] = _dot(k_ref[_kblock(j), hs], qt_ref[hs, :])

    def softmax(j, slot, h, masked):
        st = s_ref[slot, h]
        if masked:
            st = jnp.where(_chunk_mask(i, j, 1), st, NEG_INF)
        return _online_softmax(st, h, m_ref)

    def pv(j, h, alpha, p):
        _online_pv(j, h, alpha, p, acc_ref, vt_ref)

    _flash_blocks(i, heads, MLA_PV_LAG, qk, softmax, pv)
    for h in range(heads):
        acc = acc_ref[h]
        o_ref[h * MLA_V:(h + 1) * MLA_V, :] = (
            acc[0:MLA_V, :] / acc[MLA_V:MLA_V + 1, :]).astype(BF16)


def _attn_mla(qmt, km, vmt):
    t = km.shape[0]
    hg = ATTN_HEADS_PER_STEP
    groups = MLA_HEADS // hg
    nq = t // TQ
    resident = pl.Buffered(1)
    return pl.pallas_call(
        _attn_mla_kernel,
        grid=(groups, nq),
        in_specs=[
            pl.BlockSpec((hg * HEAD_PAD, TQ), lambda g, i: (g, i)),
            pl.BlockSpec((t, hg * HEAD_PAD), lambda g, i: (0, g), pipeline_mode=resident),
            pl.BlockSpec((hg, t // TK, V_ROWS, TK), lambda g, i: (g, 0, 0, 0),
                         pipeline_mode=resident),
        ],
        out_specs=pl.BlockSpec((hg * MLA_V, TQ), lambda g, i: (g, i)),
        out_shape=jax.ShapeDtypeStruct((MLA_HEADS * MLA_V, t), BF16),
        scratch_shapes=[pltpu.VMEM((2, hg, TK, TQ), F32), pltpu.VMEM((hg, 1, TQ), F32),
                        pltpu.VMEM((hg, V_ROWS, TQ), F32)],
        compiler_params=pltpu.CompilerParams(dimension_semantics=("parallel", "parallel"),
                                             vmem_limit_bytes=VMEM_LIMIT),
        name="attn_mla",
    )(qmt, km, vmt)


def _attn_diff_kernel(lam_init, qt_ref, k_ref, vt_ref, lq1_ref, lk1_ref, lq2_ref, lk2_ref,
                      gsub_ref, o_ref, wq_ref, s_ref, m_ref, acc_ref):
    i = pl.program_id(1)
    heads = vt_ref.shape[0]
    grp = lax.broadcasted_iota(jnp.int32, (128, TQ), 0) // DIFF_DC
    for pr in range(heads // 2):
        qt = qt_ref[pr * 128:(pr + 1) * 128, :]
        zero = jnp.zeros_like(qt)
        for s in range(4):
            wq_ref[pr, :, s * TQ:(s + 1) * TQ] = jnp.where(grp == s, qt, zero)
    m_ref[...] = jnp.full(m_ref.shape, NEG_INF, F32)
    acc_ref[...] = jnp.zeros(acc_ref.shape, F32)

    def qk(j, slot, h):
        pr, hh = divmod(h, 2)
        k = k_ref[_kblock(j), pr * 128:(pr + 1) * 128]
        s_ref[slot, h] = _dot(k, wq_ref[pr, :, hh * 2 * TQ:(hh + 1) * 2 * TQ])

    def softmax(j, slot, h, masked):
        st = s_ref[slot, h]
        if masked:
            st = jnp.where(_chunk_mask(i, j, 2), st, NEG_INF)
        return _online_softmax(st, h, m_ref)

    def pv(j, h, alpha, p):
        _online_pv(j, h, alpha, p, acc_ref, vt_ref)

    _flash_blocks(i, heads, DIFF_PV_LAG, qk, softmax, pv)

    lam = (jnp.exp(jnp.sum(lq1_ref[...] * lk1_ref[...], axis=-1, keepdims=True))
           - jnp.exp(jnp.sum(lq2_ref[...] * lk2_ref[...], axis=-1, keepdims=True))
           + lam_init)
    g = gsub_ref[...] * (1.0 - lam_init)
    for h in range(heads):
        acc = acc_ref[h]
        o0 = acc[0:DIFF_V, 0:TQ] / acc[DIFF_V:DIFF_V + 1, 0:TQ]
        o1 = acc[0:DIFF_V, TQ:2 * TQ] / acc[DIFF_V:DIFF_V + 1, TQ:2 * TQ]
        o = o0 - lam * o1
        o = o * lax.rsqrt(jnp.mean(o * o, axis=0, keepdims=True) + EPS) * g
        o_ref[h * DIFF_V:(h + 1) * DIFF_V, :] = o.astype(BF16)


def _attn_diff(qdt, kd, vdt, lq1, lk1, lq2, lk2, gsub, lam_init):
    t = kd.shape[0]
    nq = t // TQ
    hg = ATTN_HEADS_PER_STEP
    groups = DIFF_HEADS // hg
    vec = _full((1, DIFF_DC))
    resident = pl.Buffered(1)
    return pl.pallas_call(
        functools.partial(_attn_diff_kernel, lam_init),
        grid=(groups, nq),
        in_specs=[
            pl.BlockSpec((hg * DIFF_V, TQ), lambda g, i: (g, i)),
            pl.BlockSpec((t, hg * DIFF_V), lambda g, i: (0, g), pipeline_mode=resident),
            pl.BlockSpec((hg, t // TK, V_ROWS, TK), lambda g, i: (g, 0, 0, 0),
                         pipeline_mode=resident),
            vec, vec, vec, vec, _full((DIFF_V, 1)),
        ],
        out_specs=pl.BlockSpec((hg * DIFF_V, TQ), lambda g, i: (g, i)),
        out_shape=jax.ShapeDtypeStruct((DIFF_HEADS * DIFF_V, t), BF16),
        scratch_shapes=[pltpu.VMEM((hg // 2, 128, 4 * TQ), BF16),
                        pltpu.VMEM((2, hg, TK, 2 * TQ), F32),
                        pltpu.VMEM((hg, 1, 2 * TQ), F32),
                        pltpu.VMEM((hg, V_ROWS, 2 * TQ), F32)],
        compiler_params=pltpu.CompilerParams(dimension_semantics=("parallel", "parallel"),
                                             vmem_limit_bytes=VMEM_LIMIT),
        name="attn_diff",
    )(qdt, kd, vdt, lq1, lk1, lq2, lk2, gsub)


def _attn_mem_kernel(qt_ref, k_ref, vt_ref, o_ref):
    for h in range(MEM_HEADS):
        sl = slice(h * MEM_DH, (h + 1) * MEM_DH)
        st = _dot(k_ref[:, sl], qt_ref[sl, :])
        e = jnp.exp(st - jnp.max(st, axis=0, keepdims=True))
        p = (e / jnp.sum(e, axis=0, keepdims=True)).astype(BF16)
        o_ref[sl, :] = _dot(vt_ref[sl, :], p).astype(BF16)


def _attn_mem(qct, mk, mvt):
    w, t = qct.shape
    tq = 512
    return pl.pallas_call(
        _attn_mem_kernel,
        grid=(t // tq,),
        in_specs=[pl.BlockSpec((w, tq), lambda i: (0, i)), _full(mk.shape), _full(mvt.shape)],
        out_specs=pl.BlockSpec((w, tq), lambda i: (0, i)),
        out_shape=jax.ShapeDtypeStruct((w, t), BF16),
        compiler_params=pltpu.CompilerParams(dimension_semantics=("parallel",)),
        name="attn_mem",
    )(qct, mk, mvt)


def _softmax_two(s_a, s_b):
    m = jnp.maximum(jnp.max(s_a, axis=-1, keepdims=True), jnp.max(s_b, axis=-1, keepdims=True))
    e_a = jnp.exp(s_a - m)
    e_b = jnp.exp(s_b - m)
    inv = 1.0 / (jnp.sum(e_a, axis=-1, keepdims=True) + jnp.sum(e_b, axis=-1, keepdims=True))
    return e_a * inv, e_b * inv


def _diag_blocks(o_all, rows, width, nblk):
    lane_blk = lax.broadcasted_iota(jnp.int32, (rows, nblk * width), 1) // width
    out = jnp.zeros((rows, nblk * width), F32)
    for b in range(nblk):
        out = out + jnp.where(lane_blk == b, o_all[b * rows:(b + 1) * rows, :], 0.0)
    return out


def _sample_mla_kernel(qm_ref, ckvn_ref, krn_ref, mq_ref, ckvp_ref, krp_ref, mk_ref, mv_ref,
                       wabs_ref, wuv_ref, oa_ref, oc_ref):
    nq = qm_ref.shape[0]
    qm = qm_ref[...].astype(BF16)
    qext = jnp.concatenate(
        [_dot(qm[:, h * HEAD_PAD:(h + 1) * HEAD_PAD], wabs_ref[h]) for h in range(MLA_HEADS)],
        axis=0).astype(BF16)
    q_lat = qext[:, 0:MLA_KV_RANK]
    q_rope = qext[:, MLA_KV_RANK:MLA_KV_RANK + MLA_ROPE]
    ckv_p = ckvp_ref[...].astype(BF16)
    ckv_n = ckvn_ref[...].astype(BF16)
    s_p = _dot_nt(q_lat, ckv_p) + _dot_nt(q_rope, krp_ref[...].astype(BF16))
    s_n = _dot_nt(q_lat, ckv_n) + _dot_nt(q_rope, krn_ref[...].astype(BF16))
    p_p, p_n = _softmax_two(s_p, s_n)
    o_lat = _dot(p_p.astype(BF16), ckv_p) + _dot(p_n.astype(BF16), ckv_n)
    o_all = _dot(o_lat.astype(BF16), wuv_ref[...])
    oa_ref[...] = _diag_blocks(o_all, nq, MLA_V, MLA_HEADS)
    mq = mq_ref[...].astype(BF16)
    for h in range(MEM_HEADS):
        sl = slice(h * MEM_DH, (h + 1) * MEM_DH)
        s = _dot_nt(mq[:, sl], mk_ref[:, sl].astype(BF16))
        e = jnp.exp(s - jnp.max(s, axis=-1, keepdims=True))
        p = (e / jnp.sum(e, axis=-1, keepdims=True)).astype(BF16)
        oc_ref[:, sl] = _dot(p, mv_ref[:, sl].astype(BF16))


def _sample_mla(qm, ckv_new, kr_new, mq, ckv_past, kr_past, mem_k, mem_v, wabs, wuv, nb, nq):
    def rows(w):
        return pl.BlockSpec((nq, w), lambda b: (b, 0))

    def cache(shape):
        return pl.BlockSpec((None,) + shape, lambda b: (b, 0, 0))

    past = ckv_past.shape[1]
    return pl.pallas_call(
        _sample_mla_kernel,
        grid=(nb,),
        in_specs=[rows(1024), rows(256), rows(32), rows(512),
                  cache((past, 256)), cache((past, 32)), cache(mem_k.shape[1:]),
                  cache(mem_v.shape[1:]), _full(wabs.shape), _full(wuv.shape)],
        out_specs=[rows(512), rows(512)],
        out_shape=[jax.ShapeDtypeStruct((nb * nq, 512), F32)] * 2,
        compiler_params=pltpu.CompilerParams(dimension_semantics=("parallel",),
                                             vmem_limit_bytes=VMEM_LIMIT),
        name="sample_mla_mem",
    )(qm, ckv_new, kr_new, mq, ckv_past, kr_past, mem_k, mem_v, wabs, wuv)


def _sample_diff_kernel(lam_init, dq_ref, dkn_ref, dvn_ref, dkp_ref, dvp_ref,
                        lq1_ref, lk1_ref, lq2_ref, lk2_ref, o_ref):
    nq = dq_ref.shape[0]
    w = dq_ref.shape[1]
    nmaps = w // DIFF_DC
    q = dq_ref[...]
    row_grp = lax.broadcasted_iota(jnp.int32, (nmaps * nq, w), 0) // nq
    lane_grp = lax.broadcasted_iota(jnp.int32, (nmaps * nq, w), 1) // DIFF_DC
    qbd = jnp.where(row_grp == lane_grp, jnp.concatenate([q] * nmaps, axis=0), 0.0).astype(BF16)
    k_p = dkp_ref[...].astype(BF16)
    k_n = dkn_ref[...].astype(BF16)
    p_p, p_n = _softmax_two(_dot_nt(qbd, k_p), _dot_nt(qbd, k_n))
    lam = (jnp.exp(jnp.sum(lq1_ref[...] * lk1_ref[...], axis=-1, keepdims=True))
           - jnp.exp(jnp.sum(lq2_ref[...] * lk2_ref[...], axis=-1, keepdims=True))
           + lam_init)

    def combine(p):
        parts = [p[(2 * h) * nq:(2 * h + 1) * nq, :] - lam * p[(2 * h + 1) * nq:(2 * h + 2) * nq, :]
                 for h in range(nmaps // 2)]
        return jnp.concatenate(parts, axis=0).astype(BF16)

    o_all = (_dot(combine(p_p), dvp_ref[...].astype(BF16))
             + _dot(combine(p_n), dvn_ref[...].astype(BF16)))
    o_ref[...] = _diag_blocks(o_all, nq, DIFF_V, nmaps // 2)


def _sample_diff(dq, dk_new, dv_new, dk_past, dv_past, lq1, lk1, lq2, lk2, lam_init, nb, nq):
    hw = 256
    past = dk_past.shape[1]
    rows = pl.BlockSpec((nq, hw), lambda b, g: (b, g))
    cache = pl.BlockSpec((None, past, hw), lambda b, g: (b, 0, g))
    vec = _full((1, DIFF_DC))
    return pl.pallas_call(
        functools.partial(_sample_diff_kernel, lam_init),
        grid=(nb, 512 // hw),
        in_specs=[rows, rows, rows, cache, cache, vec, vec, vec, vec],
        out_specs=rows,
        out_shape=jax.ShapeDtypeStruct((nb * nq, 512), F32),
        compiler_params=pltpu.CompilerParams(dimension_semantics=("parallel", "parallel"),
                                             vmem_limit_bytes=VMEM_LIMIT),
        name="sample_diff",
    )(dq, dk_new, dv_new, dk_past, dv_past, lq1, lk1, lq2, lk2)


def _subln_kernel(lam_init, o_ref, g_ref, out_ref):
    g = g_ref[...] * (1.0 - lam_init)
    for h in range(DIFF_HEADS):
        o = o_ref[h * DIFF_V:(h + 1) * DIFF_V, :]
        out_ref[h * DIFF_V:(h + 1) * DIFF_V, :] = (
            o * lax.rsqrt(jnp.mean(o * o, axis=0, keepdims=True) + EPS) * g).astype(BF16)


def _subln(ot, gsub, lam_init):
    return pl.pallas_call(
        functools.partial(_subln_kernel, lam_init),
        out_shape=jax.ShapeDtypeStruct(ot.shape, BF16),
        name="sample_subln",
    )(ot, gsub)


def _mix_kernel(x_ref, oa_ref, ob_ref, oc_ref, g_ref, wg_ref, bg_ref, woa_ref, wob_ref, woc_ref,
                wout_ref, gpost_ref, y_ref):
    x = x_ref[...]
    xn = _rms(x, g_ref[...]).astype(BF16)
    d = x.shape[1]
    merged = None
    for b, (o_ref, w_ref) in enumerate(((oa_ref, woa_ref), (ob_ref, wob_ref), (oc_ref, woc_ref))):
        gate = jax.nn.sigmoid(_dot(xn, wg_ref[:, b * d:(b + 1) * d]) + bg_ref[:, b * d:(b + 1) * d])
        term = gate * _dot_tn(o_ref[...], w_ref[...])
        merged = term if merged is None else merged + term
    mix = _dot(merged.astype(BF16), wout_ref[...])
    y_ref[...] = x + _rms(mix, gpost_ref[...])


def _mix(x, oat, obt, oct, p):
    rows, d = x.shape
    tm = 512 if rows % 512 == 0 else rows
    ot_spec = pl.BlockSpec((512, tm), lambda i: (0, i))
    return pl.pallas_call(
        _mix_kernel,
        grid=(rows // tm,),
        in_specs=[_row_spec(tm, d), ot_spec, ot_spec, ot_spec, _full((1, d)),
                  _full(p['w_gate'].shape), _full((1, 3 * d)), _full((512, d)), _full((512, d)),
                  _full((512, d)), _full((d, d)), _full((1, d))],
        out_specs=_row_spec(tm, d),
        out_shape=jax.ShapeDtypeStruct((rows, d), F32),
        compiler_params=pltpu.CompilerParams(dimension_semantics=("parallel",),
                                             vmem_limit_bytes=VMEM_LIMIT),
        name="mix",
    )(x, oat, obt, oct, p['pre_mix_g'], p['w_gate'], p['b_gate'], p['w_o_mla'], p['w_o_diff'],
      p['w_o_mem'], p['w_out'], p['post_mix_g'])


def _mlp_kernel(x_ref, g_ref, wup_ref, wdn_ref, gpost_ref, y_ref):
    x = x_ref[...]
    h = _rms(x, g_ref[...]).astype(BF16)
    u = jnp.maximum(_dot(h, wup_ref[...]), 0.0)
    f = _dot((u * u).astype(BF16), wdn_ref[...])
    y_ref[...] = x + _rms(f, gpost_ref[...])


def _mlp(x, p):
    rows, d = x.shape
    tm = 256
    return pl.pallas_call(
        _mlp_kernel,
        grid=(rows // tm,),
        in_specs=[_row_spec(tm, d), _full((1, d)), _full(p['w_mlp_up'].shape),
                  _full(p['w_mlp_down'].shape), _full((1, d))],
        out_specs=_row_spec(tm, d),
        out_shape=jax.ShapeDtypeStruct((rows, d), F32),
        compiler_params=pltpu.CompilerParams(dimension_semantics=("parallel",),
                                             vmem_limit_bytes=VMEM_LIMIT),
        name="mlp",
    )(x, p['pre_mlp_g'], p['w_mlp_up'], p['w_mlp_down'], p['post_mlp_g'])


def _partner(width, group, half):
    idx = np.arange(width)
    sign = np.zeros(width, np.float32)
    d = idx % group
    first = d < half
    second = (d >= half) & (d < 2 * half)
    src = np.where(first, idx + half, np.where(second, idx - half, idx))
    sign[first] = -1.0
    sign[second] = 1.0
    return src, sign


def _take_signed(w, src, sign):
    return w[:, src] * jnp.asarray(sign)[None, :]


def _pad_cols(w, total):
    return jnp.pad(w, ((0, 0), (0, total - w.shape[1])))


def _prep_layer(l, w_in, mla_w_uq, mla_w_uk, mla_w_uv, w_mem_k, w_mem_v, w_o_mla, w_o_diff, w_o_mem,
                w_gate, b_gate, w_out, w_mlp_up, w_mlp_down, gains):
    cq, ckv, kr, dq, dk, dv, mq = jnp.split(
        w_in[l], np.cumsum((384, 256, 32, 512, 512, 512))[...].tolist(), axis=1)
    src_d, sign_d = _partner(512, DIFF_DC, DIFF_ROT // 2)
    src_r, sign_r = _partner(32, 32, MLA_ROPE // 2)
    dkp = _take_signed(dk, src_d, sign_d)
    dqp = _take_signed(dq, src_d, sign_d)
    krp = _take_signed(kr, src_r, sign_r)
    common = [cq, ckv, dk, dkp, dv, _pad_cols(kr, 128), _pad_cols(krp, 128)]
    zeros64 = jnp.zeros((D_MODEL, MLA_NOPE), F32)
    kr128 = _pad_cols(jnp.concatenate([zeros64, kr], axis=1), 128)
    kr128p = _pad_cols(jnp.concatenate([zeros64, krp], axis=1), 128)
    head_w = MLA_NOPE + MLA_ROPE
    e = np.arange(HEAD_PAD)
    valid = e < head_w
    src_q = np.concatenate([h * head_w + np.where(valid, e, 0) for h in range(MLA_HEADS)])
    sign_q = np.tile(valid.astype(np.float32), MLA_HEADS)
    in_rope1 = (e >= MLA_NOPE) & (e < MLA_NOPE + MLA_ROPE // 2)
    in_rope2 = (e >= MLA_NOPE + MLA_ROPE // 2) & valid
    pe = np.where(in_rope1, e + MLA_ROPE // 2, np.where(in_rope2, e - MLA_ROPE // 2, 0))
    src_qp = np.concatenate([h * head_w + pe for h in range(MLA_HEADS)])
    sign_qp = np.tile(np.where(in_rope1, -1.0, np.where(in_rope2, 1.0, 0.0)).astype(np.float32),
                      MLA_HEADS)
    wuq_pad = _take_signed(mla_w_uq[l], src_q, sign_q)
    wuqp_pad = _take_signed(mla_w_uq[l], src_qp, sign_qp)
    wuk = mla_w_uk[l]
    wuk_pad = jnp.pad(wuk, ((0, 0), (0, 0), (0, HEAD_PAD - MLA_NOPE))).reshape(MLA_KV_RANK, -1)
    wuv = mla_w_uv[l].reshape(MLA_KV_RANK, MLA_HEADS * MLA_V)
    sel = np.zeros((HEAD_PAD, 128), np.float32)
    sel[MLA_NOPE + np.arange(MLA_ROPE), np.arange(MLA_ROPE)] = 1.0
    wabs = jnp.concatenate([
        jnp.pad(jnp.transpose(wuk, (1, 2, 0)), ((0, 0), (0, HEAD_PAD - MLA_NOPE), (0, 0))),
        jnp.broadcast_to(jnp.asarray(sel), (MLA_HEADS, HEAD_PAD, 128))], axis=2)
    bf = lambda a: a.astype(BF16)
    row = lambda a: a[l][None, :]
    p = {
        'wn_prompt': bf(jnp.concatenate(common + [kr128, kr128p], axis=1)),
        'wn_sample': bf(jnp.concatenate(common + [dq, dqp, mq], axis=1)),
        'wt': bf(jnp.concatenate([dq, dqp, mq, dv], axis=1).T),
        'wuqt': bf(wuq_pad.T), 'wuqpt': bf(wuqp_pad.T),
        'wuq_pad': bf(wuq_pad), 'wuqp_pad': bf(wuqp_pad),
        'wuk_pad': bf(wuk_pad), 'wuvt': bf(wuv.T), 'wuv': bf(wuv), 'wabs': bf(wabs),
        'w_mem_k': bf(w_mem_k[l]), 'w_mem_v': bf(w_mem_v[l]), 'w_mem_vt': bf(w_mem_v[l].T),
        'w_o_mla': bf(w_o_mla[l]), 'w_o_diff': bf(w_o_diff[l]), 'w_o_mem': bf(w_o_mem[l]),
        'w_gate': bf(w_gate[l]), 'b_gate': row(b_gate), 'w_out': bf(w_out[l]),
        'w_mlp_up': bf(w_mlp_up[l]), 'w_mlp_down': bf(w_mlp_down[l]),
    }
    for name, g in gains.items():
        p[name] = row(g)
    return p


def _rope_cos_sin(pos, rot_dim, theta):
    half = rot_dim // 2
    inv = jnp.power(jnp.float32(theta), -jnp.arange(half, dtype=F32) * (2.0 / rot_dim))
    ang = pos.astype(F32)[:, None] * inv[None, :]
    return jnp.cos(ang), jnp.sin(ang)


def _tables(pos, reps):
    n = pos.shape[0]
    cm, sm = _rope_cos_sin(pos, MLA_ROPE, MLA_THETA)
    cd, sd = _rope_cos_sin(pos, DIFF_ROT, ROPE_THETA)
    one = lambda w: jnp.ones((n, w), F32)
    zero = lambda w: jnp.zeros((n, w), F32)
    t = {
        'c32': jnp.concatenate([cm, cm], axis=1), 's32': jnp.concatenate([sm, sm], axis=1),
        'cm': jnp.concatenate([one(64), cm, cm, one(32)], axis=1),
        'sm': jnp.concatenate([zero(64), sm, sm, zero(32)], axis=1),
        'cd': jnp.tile(jnp.concatenate([cd, cd, one(24)], axis=1), (1, 4)),
        'sd': jnp.tile(jnp.concatenate([sd, sd, zero(24)], axis=1), (1, 4)),
    }
    return {k: jnp.tile(v, (reps, 1)) for k, v in t.items()}


def kernel(x_prompt, x_sample, cache_mla_ckv, cache_mla_krope, cache_diff_k, cache_diff_v, cache_mem_k, cache_mem_v, mem_prompt, pre_mix_g, w_in, mla_q_norm_g, mla_w_uq, mla_kv_norm_g, mla_w_uk, mla_w_uv, diff_lq1, diff_lk1, diff_lq2, diff_lk2, diff_subln_g, mem_norm_g, w_mem_k, w_mem_v, w_o_mla, w_o_diff, w_o_mem, w_gate, b_gate, w_out, post_mix_g, pre_mlp_g, w_mlp_up, w_mlp_down, post_mlp_g):
    depth = w_in.shape[0]
    bp, t, d = x_prompt.shape
    nb, nq, _ = x_sample.shape
    past = cache_mla_ckv.shape[2]
    assert bp == 1 and t % TQ == 0 and TQ == 2 * TK and TK % CHUNK == 0 and d == D_MODEL
    assert past % CHUNK == 0 and nq <= CHUNK

    tabs_p = _tables(jnp.arange(t, dtype=jnp.int32), 1)
    tabs_p['cmt'] = (tabs_p['cm'] * (MLA_SCALE * LOG2E)).T
    tabs_p['smt'] = (tabs_p['sm'] * (MLA_SCALE * LOG2E)).T
    tabs_p['cdt'] = (tabs_p['cd'] * (DIFF_SCALE * LOG2E)).T
    tabs_p['sdt'] = (tabs_p['sd'] * (DIFF_SCALE * LOG2E)).T
    tabs_s = _tables(past + jnp.arange(nq, dtype=jnp.int32), nb)

    xp = x_prompt.reshape(t, d)
    xs = x_sample.reshape(nb * nq, d)
    outs = {k: [] for k in ('p_ckv', 'p_kr', 'p_dk', 'p_dv', 'p_mk', 'p_mv',
                            's_ckv', 's_kr', 's_dk', 's_dv')}
    for l in range(depth):
        lam_init = 0.8 - 0.6 * math.exp(-0.3 * l)
        p = _prep_layer(l, w_in, mla_w_uq, mla_w_uk, mla_w_uv, w_mem_k, w_mem_v, w_o_mla, w_o_diff,
                        w_o_mem, w_gate, b_gate, w_out, w_mlp_up, w_mlp_down,
                        {'pre_mix_g': pre_mix_g, 'gq': mla_q_norm_g, 'gkv': mla_kv_norm_g,
                         'mem_norm_g': mem_norm_g, 'post_mix_g': post_mix_g,
                         'pre_mlp_g': pre_mlp_g, 'post_mlp_g': post_mlp_g})
        lq1, lk1, lq2, lk2 = (a[l][None, :] for a in (diff_lq1, diff_lk1, diff_lq2, diff_lk2))
        gsub = diff_subln_g[l][:, None]

        mk, mv, mkb, mvt = _memkv(mem_prompt[0], p['mem_norm_g'], p['w_mem_k'], p['w_mem_v'],
                                  p['w_mem_vt'])
        (ckv_p, kr_p, dk_p, dv_p, kd, km, vmt, qmt, qdt, qct, vdt) = _pre_prompt(xp, p, tabs_p)
        oat = _attn_mla(qmt, km, vmt)
        obt = _attn_diff(qdt, kd, vdt, lq1, lk1, lq2, lk2, gsub, lam_init)
        oct = _attn_mem(qct, mkb, mvt)
        xp = _mlp(_mix(xp, oat, obt, oct, p), p)

        (ckv_s, kr_s, dk_s, dv_s, qm_s, dq_s, mq_s) = _pre_sample(xs, p, tabs_s)
        oa_s, oc_s = _sample_mla(qm_s, ckv_s, kr_s, mq_s, cache_mla_ckv[l], cache_mla_krope[l],
                                 cache_mem_k[l].reshape(nb, -1, 512),
                                 cache_mem_v[l].reshape(nb, -1, 512), p['wabs'], p['wuv'], nb, nq)
        ob_s = _sample_diff(dq_s, dk_s, dv_s, cache_diff_k[l].reshape(nb, past, 512),
                            cache_diff_v[l].reshape(nb, past, 512), lq1, lk1, lq2, lk2,
                            lam_init, nb, nq)
        obt_s = _subln(ob_s.T, gsub, lam_init)
        xs = _mlp(_mix(xs, oa_s.T.astype(BF16), obt_s, oc_s.T.astype(BF16), p), p)

        outs['p_ckv'].append(ckv_p.reshape(1, t, MLA_KV_RANK))
        outs['p_kr'].append(kr_p.reshape(1, t, MLA_ROPE))
        outs['p_dk'].append(dk_p.reshape(1, t, DIFF_HEADS, DIFF_V))
        outs['p_dv'].append(dv_p.reshape(1, t, DIFF_HEADS, DIFF_V))
        outs['p_mk'].append(mk.reshape(1, -1, MEM_HEADS, MEM_DH))
        outs['p_mv'].append(mv.reshape(1, -1, MEM_HEADS, MEM_DH))
        outs['s_ckv'].append(ckv_s.reshape(nb, nq, MLA_KV_RANK))
        outs['s_kr'].append(kr_s.reshape(nb, nq, MLA_ROPE))
        outs['s_dk'].append(dk_s.reshape(nb, nq, DIFF_HEADS, DIFF_V))
        outs['s_dv'].append(dv_s.reshape(nb, nq, DIFF_HEADS, DIFF_V))

    st = lambda k: jnp.stack(outs[k], axis=0)
    return (xp.reshape(1, t, d), xs.reshape(nb, nq, d),
            st('p_ckv'), st('p_kr'), st('p_dk'), st('p_dv'), st('p_mk'), st('p_mv'),
            st('s_ckv'), st('s_kr'), st('s_dk'), st('s_dv'))
```

```python
import functools
import math

import numpy as np
import jax
import jax.numpy as jnp
from jax import lax
from jax.experimental import pallas as pl
from jax.experimental.pallas import tpu as pltpu

F32 = jnp.float32
BF16 = jnp.bfloat16

D_MODEL = 1024
CHUNK = 64
EPS = 1e-6
NEG_INF = -1e30
MLA_HEADS = 8
MLA_Q_RANK = 384
MLA_KV_RANK = 256
MLA_NOPE = 64
MLA_ROPE = 32
MLA_V = 64
MLA_THETA = 10000.0
MLA_SCALE = (MLA_NOPE + MLA_ROPE) ** -0.5
DIFF_HEADS = 8
DIFF_DC = 32
DIFF_V = 64
DIFF_ROT = 8
ROPE_THETA = 500000.0
DIFF_SCALE = DIFF_DC ** -0.5
MEM_HEADS = 4
MEM_DH = 128
MEM_SCALE = MEM_DH ** -0.5
LOG2E = math.log2(math.e)
HEAD_PAD = 128
V_ROWS = 80
TQ = 512
TK = 256
ATTN_HEADS_PER_STEP = 4
MLA_PV_LAG = 1
DIFF_PV_LAG = 0
VMEM_LIMIT = 56 * 1024 * 1024

_NT = (((1,), (1,)), ((), ()))
_TN = (((0,), (0,)), ((), ()))


def _dot(a, b):
    return jnp.dot(a, b, preferred_element_type=F32)


def _dot_nt(a, b):
    return lax.dot_general(a, b, _NT, preferred_element_type=F32)


def _dot_tn(a, b):
    return lax.dot_general(a, b, _TN, preferred_element_type=F32)


def _rms(x, g):
    return x * lax.rsqrt(jnp.mean(x * x, axis=-1, keepdims=True) + EPS) * g


def _full(shape):
    nd = len(shape)
    return pl.BlockSpec(shape, lambda *_: (0,) * nd)


def _memkv_kernel(mem_ref, g_ref, wk_ref, wv_ref, wvt_ref, k_ref, v_ref, kb_ref, vt_ref):
    mn = _rms(mem_ref[...], g_ref[...]).astype(BF16)
    k = _dot(mn, wk_ref[...])
    k_ref[...] = k
    kb_ref[...] = k.astype(BF16)
    v_ref[...] = _dot(mn, wv_ref[...])
    vt_ref[...] = _dot_nt(wvt_ref[...], mn).astype(BF16)


def _memkv(mem, g, wk, wv, wvt):
    n_mem = mem.shape[0]
    w = wk.shape[1]
    return pl.pallas_call(
        _memkv_kernel,
        out_shape=(jax.ShapeDtypeStruct((n_mem, w), F32), jax.ShapeDtypeStruct((n_mem, w), F32),
                   jax.ShapeDtypeStruct((n_mem, w), BF16), jax.ShapeDtypeStruct((w, n_mem), BF16)),
        name="memkv",
    )(mem, g, wk, wv, wvt)


_N_CQ, _N_CKV, _N_DK, _N_DKP, _N_DV, _N_KR, _N_KRP = 0, 384, 640, 1152, 1664, 2176, 2304
_N_COMMON = 2432
_N_KR128, _N_KR128P = 2432, 2560
_N_PROMPT = 2688
_N_DQ, _N_DQP, _N_MQ = 2432, 2944, 3456
_N_SAMPLE = 3968


def _pre_common(x_ref, g_ref, wn_ref, gq_ref, gkv_ref, c32_ref, s32_ref, cd_ref, sd_ref,
                ckv_o, kr_o, dk_o, dv_o):
    xn = _rms(x_ref[...], g_ref[...]).astype(BF16)
    proj = _dot(xn, wn_ref[...])
    cqn = _rms(proj[:, _N_CQ:_N_CQ + MLA_Q_RANK], gq_ref[...]).astype(BF16)
    ckvn = _rms(proj[:, _N_CKV:_N_CKV + MLA_KV_RANK], gkv_ref[...])
    ckv_o[...] = ckvn
    kr_o[...] = (proj[:, _N_KR:_N_KR + MLA_ROPE] * c32_ref[...]
                 + proj[:, _N_KRP:_N_KRP + MLA_ROPE] * s32_ref[...])
    cd = cd_ref[...]
    sd = sd_ref[...]
    dk_tiles = []
    for j in range(4):
        lo = j * 128
        dk_j = (proj[:, _N_DK + lo:_N_DK + lo + 128] * cd
                + proj[:, _N_DKP + lo:_N_DKP + lo + 128] * sd)
        dk_o[:, lo:lo + 128] = dk_j
        dk_tiles.append(dk_j)
    dv_o[...] = proj[:, _N_DV:_N_DV + 512]
    return xn, proj, cqn, ckvn.astype(BF16), dk_tiles


def _pre_prompt_kernel(x_ref, g_ref, wn_ref, gq_ref, gkv_ref, c32_ref, s32_ref, cd_ref, sd_ref,
                       cm_ref, sm_ref, cmt_ref, smt_ref, cdt_ref, sdt_ref,
                       wt_ref, wuqt_ref, wuqpt_ref, wuk_ref, wuvt_ref,
                       ckv_o, kr_o, dk_o, dv_o, kd_o, km_o, vmt_o, qmt_o, qdt_o, qct_o, vdt_o):
    xn, proj, cqn, ckvb, dk_tiles = _pre_common(
        x_ref, g_ref, wn_ref, gq_ref, gkv_ref, c32_ref, s32_ref, cd_ref, sd_ref,
        ckv_o, kr_o, dk_o, dv_o)
    tm = x_ref.shape[0]
    for j in range(4):
        kd_o[:, j * 128:(j + 1) * 128] = dk_tiles[j].astype(BF16)
    kr128 = (proj[:, _N_KR128:_N_KR128 + 128] * cm_ref[...]
             + proj[:, _N_KR128P:_N_KR128P + 128] * sm_ref[...])
    knope = _dot(ckvb, wuk_ref[...])
    for h in range(MLA_HEADS):
        lo = h * HEAD_PAD
        km_o[:, lo:lo + HEAD_PAD] = (knope[:, lo:lo + HEAD_PAD] + kr128).astype(BF16)
    row = lax.broadcasted_iota(jnp.int32, (V_ROWS - MLA_V, tm), 0)
    ones_rows = jnp.where(row == 0, 1.0, 0.0).astype(BF16)
    vmt = _dot_nt(wuvt_ref[...], ckvb)
    projt = _dot_nt(wt_ref[...], xn)
    for h in range(MLA_HEADS):
        vmt_o[h, 0, 0:MLA_V, :] = vmt[h * MLA_V:(h + 1) * MLA_V, :].astype(BF16)
        vmt_o[h, 0, MLA_V:V_ROWS, :] = ones_rows
        vdt_o[h, 0, 0:DIFF_V, :] = projt[1536 + h * DIFF_V:1536 + (h + 1) * DIFF_V, :].astype(BF16)
        vdt_o[h, 0, DIFF_V:V_ROWS, :] = ones_rows
    qmt = _dot_nt(wuqt_ref[...], cqn)
    qmpt = _dot_nt(wuqpt_ref[...], cqn)
    cmt = cmt_ref[...]
    smt = smt_ref[...]
    for h in range(MLA_HEADS):
        lo = h * HEAD_PAD
        qmt_o[lo:lo + HEAD_PAD, :] = (qmt[lo:lo + HEAD_PAD, :] * cmt
                                      + qmpt[lo:lo + HEAD_PAD, :] * smt).astype(BF16)
    cdt = cdt_ref[...]
    sdt = sdt_ref[...]
    for j in range(4):
        lo = j * 128
        qdt_o[lo:lo + 128, :] = (projt[lo:lo + 128, :] * cdt
                                 + projt[512 + lo:512 + lo + 128, :] * sdt).astype(BF16)
    qct_o[...] = (projt[1024:1536, :] * MEM_SCALE).astype(BF16)


def _pre_sample_kernel(x_ref, g_ref, wn_ref, gq_ref, gkv_ref, c32_ref, s32_ref, cd_ref, sd_ref,
                       cm_ref, sm_ref, wuq_ref, wuqp_ref,
                       ckv_o, kr_o, dk_o, dv_o, qm_o, dq_o, mq_o):
    xn, proj, cqn, ckvb, dk_tiles = _pre_common(
        x_ref, g_ref, wn_ref, gq_ref, gkv_ref, c32_ref, s32_ref, cd_ref, sd_ref,
        ckv_o, kr_o, dk_o, dv_o)
    qm = _dot(cqn, wuq_ref[...])
    qmp = _dot(cqn, wuqp_ref[...])
    cm = cm_ref[...] * MLA_SCALE
    sm = sm_ref[...] * MLA_SCALE
    for h in range(MLA_HEADS):
        lo = h * HEAD_PAD
        qm_o[:, lo:lo + HEAD_PAD] = qm[:, lo:lo + HEAD_PAD] * cm + qmp[:, lo:lo + HEAD_PAD] * sm
    cd = cd_ref[...] * DIFF_SCALE
    sd = sd_ref[...] * DIFF_SCALE
    for j in range(4):
        lo = j * 128
        dq_o[:, lo:lo + 128] = (proj[:, _N_DQ + lo:_N_DQ + lo + 128] * cd
                                + proj[:, _N_DQP + lo:_N_DQP + lo + 128] * sd)
    mq_o[...] = proj[:, _N_MQ:_N_MQ + 512] * MEM_SCALE


def _row_spec(tm, w):
    return pl.BlockSpec((tm, w), lambda i: (i, 0))


def _col_spec(h, tm):
    return pl.BlockSpec((h, tm), lambda i: (0, i))


def _pre_prompt(x, p, tabs):
    rows = x.shape[0]
    tm = TK
    nkb = rows // tm
    in_specs = [
        _row_spec(tm, D_MODEL), _full((1, D_MODEL)), _full(p['wn_prompt'].shape),
        _full((1, MLA_Q_RANK)), _full((1, MLA_KV_RANK)),
        _row_spec(tm, 32), _row_spec(tm, 32), _row_spec(tm, 128), _row_spec(tm, 128),
        _row_spec(tm, 128), _row_spec(tm, 128),
        _col_spec(128, tm), _col_spec(128, tm), _col_spec(128, tm), _col_spec(128, tm),
        _full(p['wt'].shape), _full(p['wuqt'].shape), _full(p['wuqpt'].shape),
        _full(p['wuk_pad'].shape), _full(p['wuvt'].shape),
    ]
    vt_spec = pl.BlockSpec((8, 1, V_ROWS, tm), lambda i: (0, i, 0, 0))
    out_specs = [
        _row_spec(tm, 256), _row_spec(tm, 32), _row_spec(tm, 512), _row_spec(tm, 512),
        _row_spec(tm, 512), _row_spec(tm, 1024), vt_spec,
        _col_spec(1024, tm), _col_spec(512, tm), _col_spec(512, tm), vt_spec,
    ]
    out_shape = [
        jax.ShapeDtypeStruct((rows, 256), F32), jax.ShapeDtypeStruct((rows, 32), F32),
        jax.ShapeDtypeStruct((rows, 512), F32), jax.ShapeDtypeStruct((rows, 512), F32),
        jax.ShapeDtypeStruct((rows, 512), BF16), jax.ShapeDtypeStruct((rows, 1024), BF16),
        jax.ShapeDtypeStruct((8, nkb, V_ROWS, tm), BF16),
        jax.ShapeDtypeStruct((1024, rows), BF16), jax.ShapeDtypeStruct((512, rows), BF16),
        jax.ShapeDtypeStruct((512, rows), BF16),
        jax.ShapeDtypeStruct((8, nkb, V_ROWS, tm), BF16),
    ]
    return pl.pallas_call(
        _pre_prompt_kernel,
        grid=(rows // tm,),
        in_specs=in_specs, out_specs=out_specs, out_shape=out_shape,
        compiler_params=pltpu.CompilerParams(dimension_semantics=("parallel",),
                                             vmem_limit_bytes=VMEM_LIMIT),
        name="pre_prompt",
    )(x, p['pre_mix_g'], p['wn_prompt'], p['gq'], p['gkv'],
      tabs['c32'], tabs['s32'], tabs['cd'], tabs['sd'], tabs['cm'], tabs['sm'],
      tabs['cmt'], tabs['smt'], tabs['cdt'], tabs['sdt'],
      p['wt'], p['wuqt'], p['wuqpt'], p['wuk_pad'], p['wuvt'])


def _pre_sample(x, p, tabs):
    rows = x.shape[0]
    tm = 256
    in_specs = [
        _row_spec(tm, D_MODEL), _full((1, D_MODEL)), _full(p['wn_sample'].shape),
        _full((1, MLA_Q_RANK)), _full((1, MLA_KV_RANK)),
        _row_spec(tm, 32), _row_spec(tm, 32), _row_spec(tm, 128), _row_spec(tm, 128),
        _row_spec(tm, 128), _row_spec(tm, 128),
        _full(p['wuq_pad'].shape), _full(p['wuqp_pad'].shape),
    ]
    out_specs = [
        _row_spec(tm, 256), _row_spec(tm, 32), _row_spec(tm, 512), _row_spec(tm, 512),
        _row_spec(tm, 1024), _row_spec(tm, 512), _row_spec(tm, 512),
    ]
    out_shape = [
        jax.ShapeDtypeStruct((rows, 256), F32), jax.ShapeDtypeStruct((rows, 32), F32),
        jax.ShapeDtypeStruct((rows, 512), F32), jax.ShapeDtypeStruct((rows, 512), F32),
        jax.ShapeDtypeStruct((rows, 1024), F32), jax.ShapeDtypeStruct((rows, 512), F32),
        jax.ShapeDtypeStruct((rows, 512), F32),
    ]
    return pl.pallas_call(
        _pre_sample_kernel,
        grid=(rows // tm,),
        in_specs=in_specs, out_specs=out_specs, out_shape=out_shape,
        compiler_params=pltpu.CompilerParams(dimension_semantics=("parallel",),
                                             vmem_limit_bytes=VMEM_LIMIT),
        name="pre_sample",
    )(x, p['pre_mix_g'], p['wn_sample'], p['gq'], p['gkv'],
      tabs['c32'], tabs['s32'], tabs['cd'], tabs['sd'], tabs['cm'], tabs['sm'],
      p['wuq_pad'], p['wuqp_pad'])


def _chunk_mask(i, j, reps):
    kc = (j * TK + lax.broadcasted_iota(jnp.int32, (TK, TQ), 0)) // CHUNK
    qc = (i * TQ + lax.broadcasted_iota(jnp.int32, (TK, TQ), 1)) // CHUNK
    mask = kc <= qc
    return mask if reps == 1 else jnp.concatenate([mask] * reps, axis=1)


def _flash_blocks(i, heads, pv_lag, qk, softmax, pv):
    def run(steps, masked):
        pending = []
        for j_next, slot_next, j, slot in steps:
            for h in range(heads):
                if j_next is not None:
                    qk(j_next, slot_next, h)
                pending.append((j, h) + softmax(j, slot, h, masked))
                if len(pending) > pv_lag:
                    pv(*pending.pop(0))
        for item in pending:
            pv(*item)

    for h in range(heads):
        qk(0, 0, h)

    def pair(j):
        return [(j + 1, 1, j, 0), (j + 2, 0, j + 1, 1)]

    def body4(jj, c):
        run(pair(4 * jj) + pair(4 * jj + 2), False)
        return c

    def body2(jj, c):
        run(pair(2 * (i - 1)), False)
        return c

    lax.fori_loop(0, i // 2, body4, 0)
    lax.fori_loop(0, i % 2, body2, 0)
    run([(2 * i + 1, 1, 2 * i, 0), (None, None, 2 * i + 1, 1)], True)


def _online_softmax(st, h, m_ref):
    m_old = m_ref[h]
    m_new = jnp.maximum(m_old, jnp.max(st, axis=0, keepdims=True))
    m_ref[h] = m_new
    return jnp.exp2(m_old - m_new), jnp.exp2((st - m_new).astype(BF16))


def _online_pv(j, h, alpha, p, acc_ref, vt_ref):
    acc_ref[h] = alpha * acc_ref[h] + _dot(vt_ref[h, j], p)


def _kblock(j):
    return pl.ds(pl.multiple_of(j * TK, TK), TK)


def _attn_mla_kernel(qt_ref, k_ref, vt_ref, o_ref, s_ref, m_ref, acc_ref):
    i = pl.program_id(1)
    heads = qt_ref.shape[0] // HEAD_PAD
    m_ref[...] = jnp.full(m_ref.shape, NEG_INF, F32)
    acc_ref[...] = jnp.zeros(acc_ref.shape, F32)

    def qk(j, slot, h):
        hs = slice(h * HEAD_PAD, (h + 1) * HEAD_PAD)
        s_ref[slot, h] = _dot(k_ref[_kblock(j), hs], qt_ref[hs, :])

    def softmax(j, slot, h, masked):
        st = s_ref[slot, h]
        if masked:
            st = jnp.where(_chunk_mask(i, j, 1), st, NEG_INF)
        return _online_softmax(st, h, m_ref)

    def pv(j, h, alpha, p):
        _online_pv(j, h, alpha, p, acc_ref, vt_ref)

    _flash_blocks(i, heads, MLA_PV_LAG, qk, softmax, pv)
    for h in range(heads):
        acc = acc_ref[h]
        o_ref[h * MLA_V:(h + 1) * MLA_V, :] = (
            acc[0:MLA_V, :] / acc[MLA_V:MLA_V + 1, :]).astype(BF16)


def _attn_mla(qmt, km, vmt):
    t = km.shape[0]
    hg = ATTN_HEADS_PER_STEP
    groups = MLA_HEADS // hg
    nq = t // TQ
    resident = pl.Buffered(1)
    return pl.pallas_call(
        _attn_mla_kernel,
        grid=(groups, nq),
        in_specs=[
            pl.BlockSpec((hg * HEAD_PAD, TQ), lambda g, i: (g, i)),
            pl.BlockSpec((t, hg * HEAD_PAD), lambda g, i: (0, g), pipeline_mode=resident),
            pl.BlockSpec((hg, t // TK, V_ROWS, TK), lambda g, i: (g, 0, 0, 0),
                         pipeline_mode=resident),
        ],
        out_specs=pl.BlockSpec((hg * MLA_V, TQ), lambda g, i: (g, i)),
        out_shape=jax.ShapeDtypeStruct((MLA_HEADS * MLA_V, t), BF16),
        scratch_shapes=[pltpu.VMEM((2, hg, TK, TQ), F32), pltpu.VMEM((hg, 1, TQ), F32),
                        pltpu.VMEM((hg, V_ROWS, TQ), F32)],
        compiler_params=pltpu.CompilerParams(dimension_semantics=("parallel", "parallel"),
                                             vmem_limit_bytes=VMEM_LIMIT),
        name="attn_mla",
    )(qmt, km, vmt)


def _attn_diff_kernel(lam_init, qt_ref, k_ref, vt_ref, lq1_ref, lk1_ref, lq2_ref, lk2_ref,
                      gsub_ref, o_ref, wq_ref, s_ref, m_ref, acc_ref):
    i = pl.program_id(1)
    heads = vt_ref.shape[0]
    grp = lax.broadcasted_iota(jnp.int32, (128, TQ), 0) // DIFF_DC
    for pr in range(heads // 2):
        qt = qt_ref[pr * 128:(pr + 1) * 128, :]
        zero = jnp.zeros_like(qt)
        for s in range(4):
            wq_ref[pr, :, s * TQ:(s + 1) * TQ] = jnp.where(grp == s, qt, zero)
    m_ref[...] = jnp.full(m_ref.shape, NEG_INF, F32)
    acc_ref[...] = jnp.zeros(acc_ref.shape, F32)

    def qk(j, slot, h):
        pr, hh = divmod(h, 2)
        k = k_ref[_kblock(j), pr * 128:(pr + 1) * 128]
        s_ref[slot, h] = _dot(k, wq_ref[pr, :, hh * 2 * TQ:(hh + 1) * 2 * TQ])

    def softmax(j, slot, h, masked):
        st = s_ref[slot, h]
        if masked:
            st = jnp.where(_chunk_mask(i, j, 2), st, NEG_INF)
        return _online_softmax(st, h, m_ref)

    def pv(j, h, alpha, p):
        _online_pv(j, h, alpha, p, acc_ref, vt_ref)

    _flash_blocks(i, heads, DIFF_PV_LAG, qk, softmax, pv)

    lam = (jnp.exp(jnp.sum(lq1_ref[...] * lk1_ref[...], axis=-1, keepdims=True))
           - jnp.exp(jnp.sum(lq2_ref[...] * lk2_ref[...], axis=-1, keepdims=True))
           + lam_init)
    g = gsub_ref[...] * (1.0 - lam_init)
    for h in range(heads):
        acc = acc_ref[h]
        o0 = acc[0:DIFF_V, 0:TQ] / acc[DIFF_V:DIFF_V + 1, 0:TQ]
        o1 = acc[0:DIFF_V, TQ:2 * TQ] / acc[DIFF_V:DIFF_V + 1, TQ:2 * TQ]
        o = o0 - lam * o1
        o = o * lax.rsqrt(jnp.mean(o * o, axis=0, keepdims=True) + EPS) * g
        o_ref[h * DIFF_V:(h + 1) * DIFF_V, :] = o.astype(BF16)


def _attn_diff(qdt, kd, vdt, lq1, lk1, lq2, lk2, gsub, lam_init):
    t = kd.shape[0]
    nq = t // TQ
    hg = ATTN_HEADS_PER_STEP
    groups = DIFF_HEADS // hg
    vec = _full((1, DIFF_DC))
    resident = pl.Buffered(1)
    return pl.pallas_call(
        functools.partial(_attn_diff_kernel, lam_init),
        grid=(groups, nq),
        in_specs=[
            pl.BlockSpec((hg * DIFF_V, TQ), lambda g, i: (g, i)),
            pl.BlockSpec((t, hg * DIFF_V), lambda g, i: (0, g), pipeline_mode=resident),
            pl.BlockSpec((hg, t // TK, V_ROWS, TK), lambda g, i: (g, 0, 0, 0),
                         pipeline_mode=resident),
            vec, vec, vec, vec, _full((DIFF_V, 1)),
        ],
        out_specs=pl.BlockSpec((hg * DIFF_V, TQ), lambda g, i: (g, i)),
        out_shape=jax.ShapeDtypeStruct((DIFF_HEADS * DIFF_V, t), BF16),
        scratch_shapes=[pltpu.VMEM((hg // 2, 128, 4 * TQ), BF16),
                        pltpu.VMEM((2, hg, TK, 2 * TQ), F32),
                        pltpu.VMEM((hg, 1, 2 * TQ), F32),
                        pltpu.VMEM((hg, V_ROWS, 2 * TQ), F32)],
        compiler_params=pltpu.CompilerParams(dimension_semantics=("parallel", "parallel"),
                                             vmem_limit_bytes=VMEM_LIMIT),
        name="attn_diff",
    )(qdt, kd, vdt, lq1, lk1, lq2, lk2, gsub)


def _attn_mem_kernel(qt_ref, k_ref, vt_ref, o_ref):
    for h in range(MEM_HEADS):
        sl = slice(h * MEM_DH, (h + 1) * MEM_DH)
        st = _dot(k_ref[:, sl], qt_ref[sl, :])
        e = jnp.exp(st - jnp.max(st, axis=0, keepdims=True))
        p = (e / jnp.sum(e, axis=0, keepdims=True)).astype(BF16)
        o_ref[sl, :] = _dot(vt_ref[sl, :], p).astype(BF16)


def _attn_mem(qct, mk, mvt):
    w, t = qct.shape
    tq = 512
    return pl.pallas_call(
        _attn_mem_kernel,
        grid=(t // tq,),
        in_specs=[pl.BlockSpec((w, tq), lambda i: (0, i)), _full(mk.shape), _full(mvt.shape)],
        out_specs=pl.BlockSpec((w, tq), lambda i: (0, i)),
        out_shape=jax.ShapeDtypeStruct((w, t), BF16),
        compiler_params=pltpu.CompilerParams(dimension_semantics=("parallel",)),
        name="attn_mem",
    )(qct, mk, mvt)


def _softmax_two(s_a, s_b):
    m = jnp.maximum(jnp.max(s_a, axis=-1, keepdims=True), jnp.max(s_b, axis=-1, keepdims=True))
    e_a = jnp.exp(s_a - m)
    e_b = jnp.exp(s_b - m)
    inv = 1.0 / (jnp.sum(e_a, axis=-1, keepdims=True) + jnp.sum(e_b, axis=-1, keepdims=True))
    return e_a * inv, e_b * inv


def _diag_blocks(o_all, rows, width, nblk):
    lane_blk = lax.broadcasted_iota(jnp.int32, (rows, nblk * width), 1) // width
    out = jnp.zeros((rows, nblk * width), F32)
    for b in range(nblk):
        out = out + jnp.where(lane_blk == b, o_all[b * rows:(b + 1) * rows, :], 0.0)
    return out


def _sample_mla_kernel(qm_ref, ckvn_ref, krn_ref, mq_ref, ckvp_ref, krp_ref, mk_ref, mv_ref,
                       wabs_ref, wuv_ref, oa_ref, oc_ref):
    nq = qm_ref.shape[0]
    qm = qm_ref[...].astype(BF16)
    qext = jnp.concatenate(
        [_dot(qm[:, h * HEAD_PAD:(h + 1) * HEAD_PAD], wabs_ref[h]) for h in range(MLA_HEADS)],
        axis=0).astype(BF16)
    q_lat = qext[:, 0:MLA_KV_RANK]
    q_rope = qext[:, MLA_KV_RANK:MLA_KV_RANK + MLA_ROPE]
    ckv_p = ckvp_ref[...].astype(BF16)
    ckv_n = ckvn_ref[...].astype(BF16)
    s_p = _dot_nt(q_lat, ckv_p) + _dot_nt(q_rope, krp_ref[...].astype(BF16))
    s_n = _dot_nt(q_lat, ckv_n) + _dot_nt(q_rope, krn_ref[...].astype(BF16))
    p_p, p_n = _softmax_two(s_p, s_n)
    o_lat = _dot(p_p.astype(BF16), ckv_p) + _dot(p_n.astype(BF16), ckv_n)
    o_all = _dot(o_lat.astype(BF16), wuv_ref[...])
    oa_ref[...] = _diag_blocks(o_all, nq, MLA_V, MLA_HEADS)
    mq = mq_ref[...].astype(BF16)
    for h in range(MEM_HEADS):
        sl = slice(h * MEM_DH, (h + 1) * MEM_DH)
        s = _dot_nt(mq[:, sl], mk_ref[:, sl].astype(BF16))
        e = jnp.exp(s - jnp.max(s, axis=-1, keepdims=True))
        p = (e / jnp.sum(e, axis=-1, keepdims=True)).astype(BF16)
        oc_ref[:, sl] = _dot(p, mv_ref[:, sl].astype(BF16))


def _sample_mla(qm, ckv_new, kr_new, mq, ckv_past, kr_past, mem_k, mem_v, wabs, wuv, nb, nq):
    def rows(w):
        return pl.BlockSpec((nq, w), lambda b: (b, 0))

    def cache(shape):
        return pl.BlockSpec((None,) + shape, lambda b: (b, 0, 0))

    past = ckv_past.shape[1]
    return pl.pallas_call(
        _sample_mla_kernel,
        grid=(nb,),
        in_specs=[rows(1024), rows(256), rows(32), rows(512),
                  cache((past, 256)), cache((past, 32)), cache(mem_k.shape[1:]),
                  cache(mem_v.shape[1:]), _full(wabs.shape), _full(wuv.shape)],
        out_specs=[rows(512), rows(512)],
        out_shape=[jax.ShapeDtypeStruct((nb * nq, 512), F32)] * 2,
        compiler_params=pltpu.CompilerParams(dimension_semantics=("parallel",),
                                             vmem_limit_bytes=VMEM_LIMIT),
        name="sample_mla_mem",
    )(qm, ckv_new, kr_new, mq, ckv_past, kr_past, mem_k, mem_v, wabs, wuv)


def _sample_diff_kernel(lam_init, dq_ref, dkn_ref, dvn_ref, dkp_ref, dvp_ref,
                        lq1_ref, lk1_ref, lq2_ref, lk2_ref, o_ref):
    nq = dq_ref.shape[0]
    w = dq_ref.shape[1]
    nmaps = w // DIFF_DC
    q = dq_ref[...]
    row_grp = lax.broadcasted_iota(jnp.int32, (nmaps * nq, w), 0) // nq
    lane_grp = lax.broadcasted_iota(jnp.int32, (nmaps * nq, w), 1) // DIFF_DC
    qbd = jnp.where(row_grp == lane_grp, jnp.concatenate([q] * nmaps, axis=0), 0.0).astype(BF16)
    k_p = dkp_ref[...].astype(BF16)
    k_n = dkn_ref[...].astype(BF16)
    p_p, p_n = _softmax_two(_dot_nt(qbd, k_p), _dot_nt(qbd, k_n))
    lam = (jnp.exp(jnp.sum(lq1_ref[...] * lk1_ref[...], axis=-1, keepdims=True))
           - jnp.exp(jnp.sum(lq2_ref[...] * lk2_ref[...], axis=-1, keepdims=True))
           + lam_init)

    def combine(p):
        parts = [p[(2 * h) * nq:(2 * h + 1) * nq, :] - lam * p[(2 * h + 1) * nq:(2 * h + 2) * nq, :]
                 for h in range(nmaps // 2)]
        return jnp.concatenate(parts, axis=0).astype(BF16)

    o_all = (_dot(combine(p_p), dvp_ref[...].astype(BF16))
             + _dot(combine(p_n), dvn_ref[...].astype(BF16)))
    o_ref[...] = _diag_blocks(o_all, nq, DIFF_V, nmaps // 2)


def _sample_diff(dq, dk_new, dv_new, dk_past, dv_past, lq1, lk1, lq2, lk2, lam_init, nb, nq):
    hw = 256
    past = dk_past.shape[1]
    rows = pl.BlockSpec((nq, hw), lambda b, g: (b, g))
    cache = pl.BlockSpec((None, past, hw), lambda b, g: (b, 0, g))
    vec = _full((1, DIFF_DC))
    return pl.pallas_call(
        functools.partial(_sample_diff_kernel, lam_init),
        grid=(nb, 512 // hw),
        in_specs=[rows, rows, rows, cache, cache, vec, vec, vec, vec],
        out_specs=rows,
        out_shape=jax.ShapeDtypeStruct((nb * nq, 512), F32),
        compiler_params=pltpu.CompilerParams(dimension_semantics=("parallel", "parallel"),
                                             vmem_limit_bytes=VMEM_LIMIT),
        name="sample_diff",
    )(dq, dk_new, dv_new, dk_past, dv_past, lq1, lk1, lq2, lk2)


def _subln_kernel(lam_init, o_ref, g_ref, out_ref):
    g = g_ref[...] * (1.0 - lam_init)
    for h in range(DIFF_HEADS):
        o = o_ref[h * DIFF_V:(h + 1) * DIFF_V, :]
        out_ref[h * DIFF_V:(h + 1) * DIFF_V, :] = (
            o * lax.rsqrt(jnp.mean(o * o, axis=0, keepdims=True) + EPS) * g).astype(BF16)


def _subln(ot, gsub, lam_init):
    return pl.pallas_call(
        functools.partial(_subln_kernel, lam_init),
        out_shape=jax.ShapeDtypeStruct(ot.shape, BF16),
        name="sample_subln",
    )(ot, gsub)


def _mix_kernel(x_ref, oa_ref, ob_ref, oc_ref, g_ref, wg_ref, bg_ref, woa_ref, wob_ref, woc_ref,
                wout_ref, gpost_ref, y_ref):
    x = x_ref[...]
    xn = _rms(x, g_ref[...]).astype(BF16)
    d = x.shape[1]
    merged = None
    for b, (o_ref, w_ref) in enumerate(((oa_ref, woa_ref), (ob_ref, wob_ref), (oc_ref, woc_ref))):
        gate = jax.nn.sigmoid(_dot(xn, wg_ref[:, b * d:(b + 1) * d]) + bg_ref[:, b * d:(b + 1) * d])
        term = gate * _dot_tn(o_ref[...], w_ref[...])
        merged = term if merged is None else merged + term
    mix = _dot(merged.astype(BF16), wout_ref[...])
    y_ref[...] = x + _rms(mix, gpost_ref[...])


def _mix(x, oat, obt, oct, p):
    rows, d = x.shape
    tm = 512 if rows % 512 == 0 else rows
    ot_spec = pl.BlockSpec((512, tm), lambda i: (0, i))
    return pl.pallas_call(
        _mix_kernel,
        grid=(rows // tm,),
        in_specs=[_row_spec(tm, d), ot_spec, ot_spec, ot_spec, _full((1, d)),
                  _full(p['w_gate'].shape), _full((1, 3 * d)), _full((512, d)), _full((512, d)),
                  _full((512, d)), _full((d, d)), _full((1, d))],
        out_specs=_row_spec(tm, d),
        out_shape=jax.ShapeDtypeStruct((rows, d), F32),
        compiler_params=pltpu.CompilerParams(dimension_semantics=("parallel",),
                                             vmem_limit_bytes=VMEM_LIMIT),
        name="mix",
    )(x, oat, obt, oct, p['pre_mix_g'], p['w_gate'], p['b_gate'], p['w_o_mla'], p['w_o_diff'],
      p['w_o_mem'], p['w_out'], p['post_mix_g'])


def _mlp_kernel(x_ref, g_ref, wup_ref, wdn_ref, gpost_ref, y_ref):
    x = x_ref[...]
    h = _rms(x, g_ref[...]).astype(BF16)
    u = jnp.maximum(_dot(h, wup_ref[...]), 0.0)
    f = _dot((u * u).astype(BF16), wdn_ref[...])
    y_ref[...] = x + _rms(f, gpost_ref[...])


def _mlp(x, p):
    rows, d = x.shape
    tm = 256
    return pl.pallas_call(
        _mlp_kernel,
        grid=(rows // tm,),
        in_specs=[_row_spec(tm, d), _full((1, d)), _full(p['w_mlp_up'].shape),
                  _full(p['w_mlp_down'].shape), _full((1, d))],
        out_specs=_row_spec(tm, d),
        out_shape=jax.ShapeDtypeStruct((rows, d), F32),
        compiler_params=pltpu.CompilerParams(dimension_semantics=("parallel",),
                                             vmem_limit_bytes=VMEM_LIMIT),
        name="mlp",
    )(x, p['pre_mlp_g'], p['w_mlp_up'], p['w_mlp_down'], p['post_mlp_g'])


def _partner(width, group, half):
    idx = np.arange(width)
    sign = np.zeros(width, np.float32)
    d = idx % group
    first = d < half
    second = (d >= half) & (d < 2 * half)
    src = np.where(first, idx + half, np.where(second, idx - half, idx))
    sign[first] = -1.0
    sign[second] = 1.0
    return src, sign


def _take_signed(w, src, sign):
    return w[:, src] * jnp.asarray(sign)[None, :]


def _pad_cols(w, total):
    return jnp.pad(w, ((0, 0), (0, total - w.shape[1])))


def _prep_layer(l, w_in, mla_w_uq, mla_w_uk, mla_w_uv, w_mem_k, w_mem_v, w_o_mla, w_o_diff, w_o_mem,
                w_gate, b_gate, w_out, w_mlp_up, w_mlp_down, gains):
    cq, ckv, kr, dq, dk, dv, mq = jnp.split(
        w_in[l], np.cumsum((384, 256, 32, 512, 512, 512))[...].tolist(), axis=1)
    src_d, sign_d = _partner(512, DIFF_DC, DIFF_ROT // 2)
    src_r, sign_r = _partner(32, 32, MLA_ROPE // 2)
    dkp = _take_signed(dk, src_d, sign_d)
    dqp = _take_signed(dq, src_d, sign_d)
    krp = _take_signed(kr, src_r, sign_r)
    common = [cq, ckv, dk, dkp, dv, _pad_cols(kr, 128), _pad_cols(krp, 128)]
    zeros64 = jnp.zeros((D_MODEL, MLA_NOPE), F32)
    kr128 = _pad_cols(jnp.concatenate([zeros64, kr], axis=1), 128)
    kr128p = _pad_cols(jnp.concatenate([zeros64, krp], axis=1), 128)
    head_w = MLA_NOPE + MLA_ROPE
    e = np.arange(HEAD_PAD)
    valid = e < head_w
    src_q = np.concatenate([h * head_w + np.where(valid, e, 0) for h in range(MLA_HEADS)])
    sign_q = np.tile(valid.astype(np.float32), MLA_HEADS)
    in_rope1 = (e >= MLA_NOPE) & (e < MLA_NOPE + MLA_ROPE // 2)
    in_rope2 = (e >= MLA_NOPE + MLA_ROPE // 2) & valid
    pe = np.where(in_rope1, e + MLA_ROPE // 2, np.where(in_rope2, e - MLA_ROPE // 2, 0))
    src_qp = np.concatenate([h * head_w + pe for h in range(MLA_HEADS)])
    sign_qp = np.tile(np.where(in_rope1, -1.0, np.where(in_rope2, 1.0, 0.0)).astype(np.float32),
                      MLA_HEADS)
    wuq_pad = _take_signed(mla_w_uq[l], src_q, sign_q)
    wuqp_pad = _take_signed(mla_w_uq[l], src_qp, sign_qp)
    wuk = mla_w_uk[l]
    wuk_pad = jnp.pad(wuk, ((0, 0), (0, 0), (0, HEAD_PAD - MLA_NOPE))).reshape(MLA_KV_RANK, -1)
    wuv = mla_w_uv[l].reshape(MLA_KV_RANK, MLA_HEADS * MLA_V)
    sel = np.zeros((HEAD_PAD, 128), np.float32)
    sel[MLA_NOPE + np.arange(MLA_ROPE), np.arange(MLA_ROPE)] = 1.0
    wabs = jnp.concatenate([
        jnp.pad(jnp.transpose(wuk, (1, 2, 0)), ((0, 0), (0, HEAD_PAD - MLA_NOPE), (0, 0))),
        jnp.broadcast_to(jnp.asarray(sel), (MLA_HEADS, HEAD_PAD, 128))], axis=2)
    bf = lambda a: a.astype(BF16)
    row = lambda a: a[l][None, :]
    p = {
        'wn_prompt': bf(jnp.concatenate(common + [kr128, kr128p], axis=1)),
        'wn_sample': bf(jnp.concatenate(common + [dq, dqp, mq], axis=1)),
        'wt': bf(jnp.concatenate([dq, dqp, mq, dv], axis=1).T),
        'wuqt': bf(wuq_pad.T), 'wuqpt': bf(wuqp_pad.T),
        'wuq_pad': bf(wuq_pad), 'wuqp_pad': bf(wuqp_pad),
        'wuk_pad': bf(wuk_pad), 'wuvt': bf(wuv.T), 'wuv': bf(wuv), 'wabs': bf(wabs),
        'w_mem_k': bf(w_mem_k[l]), 'w_mem_v': bf(w_mem_v[l]), 'w_mem_vt': bf(w_mem_v[l].T),
        'w_o_mla': bf(w_o_mla[l]), 'w_o_diff': bf(w_o_diff[l]), 'w_o_mem': bf(w_o_mem[l]),
        'w_gate': bf(w_gate[l]), 'b_gate': row(b_gate), 'w_out': bf(w_out[l]),
        'w_mlp_up': bf(w_mlp_up[l]), 'w_mlp_down': bf(w_mlp_down[l]),
    }
    for name, g in gains.items():
        p[name] = row(g)
    return p


def _rope_cos_sin(pos, rot_dim, theta):
    half = rot_dim // 2
    inv = jnp.power(jnp.float32(theta), -jnp.arange(half, dtype=F32) * (2.0 / rot_dim))
    ang = pos.astype(F32)[:, None] * inv[None, :]
    return jnp.cos(ang), jnp.sin(ang)


def _tables(pos, reps):
    n = pos.shape[0]
    cm, sm = _rope_cos_sin(pos, MLA_ROPE, MLA_THETA)
    cd, sd = _rope_cos_sin(pos, DIFF_ROT, ROPE_THETA)
    one = lambda w: jnp.ones((n, w), F32)
    zero = lambda w: jnp.zeros((n, w), F32)
    t = {
        'c32': jnp.concatenate([cm, cm], axis=1), 's32': jnp.concatenate([sm, sm], axis=1),
        'cm': jnp.concatenate([one(64), cm, cm, one(32)], axis=1),
        'sm': jnp.concatenate([zero(64), sm, sm, zero(32)], axis=1),
        'cd': jnp.tile(jnp.concatenate([cd, cd, one(24)], axis=1), (1, 4)),
        'sd': jnp.tile(jnp.concatenate([sd, sd, zero(24)], axis=1), (1, 4)),
    }
    return {k: jnp.tile(v, (reps, 1)) for k, v in t.items()}


def kernel(x_prompt, x_sample, cache_mla_ckv, cache_mla_krope, cache_diff_k, cache_diff_v, cache_mem_k, cache_mem_v, mem_prompt, pre_mix_g, w_in, mla_q_norm_g, mla_w_uq, mla_kv_norm_g, mla_w_uk, mla_w_uv, diff_lq1, diff_lk1, diff_lq2, diff_lk2, diff_subln_g, mem_norm_g, w_mem_k, w_mem_v, w_o_mla, w_o_diff, w_o_mem, w_gate, b_gate, w_out, post_mix_g, pre_mlp_g, w_mlp_up, w_mlp_down, post_mlp_g):
    depth = w_in.shape[0]
    bp, t, d = x_prompt.shape
    nb, nq, _ = x_sample.shape
    past = cache_mla_ckv.shape[2]
    assert bp == 1 and t % TQ == 0 and TQ == 2 * TK and TK % CHUNK == 0 and d == D_MODEL
    assert past % CHUNK == 0 and nq <= CHUNK

    tabs_p = _tables(jnp.arange(t, dtype=jnp.int32), 1)
    tabs_p['cmt'] = (tabs_p['cm'] * (MLA_SCALE * LOG2E)).T
    tabs_p['smt'] = (tabs_p['sm'] * (MLA_SCALE * LOG2E)).T
    tabs_p['cdt'] = (tabs_p['cd'] * (DIFF_SCALE * LOG2E)).T
    tabs_p['sdt'] = (tabs_p['sd'] * (DIFF_SCALE * LOG2E)).T
    tabs_s = _tables(past + jnp.arange(nq, dtype=jnp.int32), nb)

    xp = x_prompt.reshape(t, d)
    xs = x_sample.reshape(nb * nq, d)
    outs = {k: [] for k in ('p_ckv', 'p_kr', 'p_dk', 'p_dv', 'p_mk', 'p_mv',
                            's_ckv', 's_kr', 's_dk', 's_dv')}
    for l in range(depth):
        lam_init = 0.8 - 0.6 * math.exp(-0.3 * l)
        p = _prep_layer(l, w_in, mla_w_uq, mla_w_uk, mla_w_uv, w_mem_k, w_mem_v, w_o_mla, w_o_diff,
                        w_o_mem, w_gate, b_gate, w_out, w_mlp_up, w_mlp_down,
                        {'pre_mix_g': pre_mix_g, 'gq': mla_q_norm_g, 'gkv': mla_kv_norm_g,
                         'mem_norm_g': mem_norm_g, 'post_mix_g': post_mix_g,
                         'pre_mlp_g': pre_mlp_g, 'post_mlp_g': post_mlp_g})
        lq1, lk1, lq2, lk2 = (a[l][None, :] for a in (diff_lq1, diff_lk1, diff_lq2, diff_lk2))
        gsub = diff_subln_g[l][:, None]

        mk, mv, mkb, mvt = _memkv(mem_prompt[0], p['mem_norm_g'], p['w_mem_k'], p['w_mem_v'],
                                  p['w_mem_vt'])
        (ckv_p, kr_p, dk_p, dv_p, kd, km, vmt, qmt, qdt, qct, vdt) = _pre_prompt(xp, p, tabs_p)
        oat = _attn_mla(qmt, km, vmt)
        obt = _attn_diff(qdt, kd, vdt, lq1, lk1, lq2, lk2, gsub, lam_init)
        oct = _attn_mem(qct, mkb, mvt)
        xp = _mlp(_mix(xp, oat, obt, oct, p), p)

        (ckv_s, kr_s, dk_s, dv_s, qm_s, dq_s, mq_s) = _pre_sample(xs, p, tabs_s)
        oa_s, oc_s = _sample_mla(qm_s, ckv_s, kr_s, mq_s, cache_mla_ckv[l], cache_mla_krope[l],
                                 cache_mem_k[l].reshape(nb, -1, 512),
                                 cache_mem_v[l].reshape(nb, -1, 512), p['wabs'], p['wuv'], nb, nq)
        ob_s = _sample_diff(dq_s, dk_s, dv_s, cache_diff_k[l].reshape(nb, past, 512).astype(BF16),
                            cache_diff_v[l].reshape(nb, past, 512).astype(BF16), lq1, lk1, lq2, lk2,
                            lam_init, nb, nq)
        obt_s = _subln(ob_s.T, gsub, lam_init)
        xs = _mlp(_mix(xs, oa_s.T.astype(BF16), obt_s, oc_s.T.astype(BF16), p), p)

        outs['p_ckv'].append(ckv_p.reshape(1, t, MLA_KV_RANK))
        outs['p_kr'].append(kr_p.reshape(1, t, MLA_ROPE))
        outs['p_dk'].append(dk_p.reshape(1, t, DIFF_HEADS, DIFF_V))
        outs['p_dv'].append(dv_p.reshape(1, t, DIFF_HEADS, DIFF_V))
        outs['p_mk'].append(mk.reshape(1, -1, MEM_HEADS, MEM_DH))
        outs['p_mv'].append(mv.reshape(1, -1, MEM_HEADS, MEM_DH))
        outs['s_ckv'].append(ckv_s.reshape(nb, nq, MLA_KV_RANK))
        outs['s_kr'].append(kr_s.reshape(nb, nq, MLA_ROPE))
        outs['s_dk'].append(dk_s.reshape(nb, nq, DIFF_HEADS, DIFF_V))
        outs['s_dv'].append(dv_s.reshape(nb, nq, DIFF_HEADS, DIFF_V))

    st = lambda k: jnp.stack(outs[k], axis=0)
    return (xp.reshape(1, t, d), xs.reshape(nb, nq, d),
            st('p_ckv'), st('p_kr'), st('p_dk'), st('p_dv'), st('p_mk'), st('p_mv'),
            st('s_ckv'), st('s_kr'), st('s_dk'), st('s_dv'))
```

```python
import functools
import math

import numpy as np
import jax
import jax.numpy as jnp
from jax import lax
from jax.experimental import pallas as pl
from jax.experimental.pallas import tpu as pltpu

F32 = jnp.float32
BF16 = jnp.bfloat16

D_MODEL = 1024
CHUNK = 64
EPS = 1e-6
NEG_INF = -1e30
MLA_HEADS = 8
MLA_Q_RANK = 384
MLA_KV_RANK = 256
MLA_NOPE = 64
MLA_ROPE = 32
MLA_V = 64
MLA_THETA = 10000.0
MLA_SCALE = (MLA_NOPE + MLA_ROPE) ** -0.5
DIFF_HEADS = 8
DIFF_DC = 32
DIFF_V = 64
DIFF_ROT = 8
ROPE_THETA = 500000.0
DIFF_SCALE = DIFF_DC ** -0.5
MEM_HEADS = 4
MEM_DH = 128
MEM_SCALE = MEM_DH ** -0.5
LOG2E = math.log2(math.e)
HEAD_PAD = 128
V_ROWS = 80
TQ = 512
TK = 256
ATTN_HEADS_PER_STEP = 4
MLA_PV_LAG = 1
DIFF_PV_LAG = 0
VMEM_LIMIT = 56 * 1024 * 1024

_NT = (((1,), (1,)), ((), ()))
_TN = (((0,), (0,)), ((), ()))


def _dot(a, b):
    return jnp.dot(a, b, preferred_element_type=F32)


def _dot_nt(a, b):
    return lax.dot_general(a, b, _NT, preferred_element_type=F32)


def _dot_tn(a, b):
    return lax.dot_general(a, b, _TN, preferred_element_type=F32)


def _rms(x, g):
    return x * lax.rsqrt(jnp.mean(x * x, axis=-1, keepdims=True) + EPS) * g


def _full(shape):
    nd = len(shape)
    return pl.BlockSpec(shape, lambda *_: (0,) * nd)


def _memkv_kernel(mem_ref, g_ref, wk_ref, wv_ref, wvt_ref, k_ref, v_ref, kb_ref, vt_ref):
    mn = _rms(mem_ref[...], g_ref[...]).astype(BF16)
    k = _dot(mn, wk_ref[...])
    k_ref[...] = k
    kb_ref[...] = k.astype(BF16)
    v_ref[...] = _dot(mn, wv_ref[...])
    vt_ref[...] = _dot_nt(wvt_ref[...], mn).astype(BF16)


def _memkv(mem, g, wk, wv, wvt):
    n_mem = mem.shape[0]
    w = wk.shape[1]
    return pl.pallas_call(
        _memkv_kernel,
        out_shape=(jax.ShapeDtypeStruct((n_mem, w), F32), jax.ShapeDtypeStruct((n_mem, w), F32),
                   jax.ShapeDtypeStruct((n_mem, w), BF16), jax.ShapeDtypeStruct((w, n_mem), BF16)),
        name="memkv",
    )(mem, g, wk, wv, wvt)


_N_CQ, _N_CKV, _N_DK, _N_DKP, _N_DV, _N_KR, _N_KRP = 0, 384, 640, 1152, 1664, 2176, 2304
_N_COMMON = 2432
_N_KR128, _N_KR128P = 2432, 2560
_N_PROMPT = 2688
_N_DQ, _N_DQP, _N_MQ = 2432, 2944, 3456
_N_SAMPLE = 3968


def _pre_common(x_ref, g_ref, wn_ref, gq_ref, gkv_ref, c32_ref, s32_ref, cd_ref, sd_ref,
                ckv_o, kr_o, dk_o, dv_o):
    xn = _rms(x_ref[...], g_ref[...]).astype(BF16)
    proj = _dot(xn, wn_ref[...])
    cqn = _rms(proj[:, _N_CQ:_N_CQ + MLA_Q_RANK], gq_ref[...]).astype(BF16)
    ckvn = _rms(proj[:, _N_CKV:_N_CKV + MLA_KV_RANK], gkv_ref[...])
    ckv_o[...] = ckvn
    kr_o[...] = (proj[:, _N_KR:_N_KR + MLA_ROPE] * c32_ref[...]
                 + proj[:, _N_KRP:_N_KRP + MLA_ROPE] * s32_ref[...])
    cd = cd_ref[...]
    sd = sd_ref[...]
    dk_tiles = []
    for j in range(4):
        lo = j * 128
        dk_j = (proj[:, _N_DK + lo:_N_DK + lo + 128] * cd
                + proj[:, _N_DKP + lo:_N_DKP + lo + 128] * sd)
        dk_o[:, lo:lo + 128] = dk_j
        dk_tiles.append(dk_j)
    dv_o[...] = proj[:, _N_DV:_N_DV + 512]
    return xn, proj, cqn, ckvn.astype(BF16), dk_tiles


def _pre_prompt_kernel(x_ref, g_ref, wn_ref, gq_ref, gkv_ref, c32_ref, s32_ref, cd_ref, sd_ref,
                       cm_ref, sm_ref, cmt_ref, smt_ref, cdt_ref, sdt_ref,
                       wt_ref, wuqt_ref, wuqpt_ref, wuk_ref, wuvt_ref,
                       ckv_o, kr_o, dk_o, dv_o, kd_o, km_o, vmt_o, qmt_o, qdt_o, qct_o, vdt_o):
    xn, proj, cqn, ckvb, dk_tiles = _pre_common(
        x_ref, g_ref, wn_ref, gq_ref, gkv_ref, c32_ref, s32_ref, cd_ref, sd_ref,
        ckv_o, kr_o, dk_o, dv_o)
    tm = x_ref.shape[0]
    for j in range(4):
        kd_o[:, j * 128:(j + 1) * 128] = dk_tiles[j].astype(BF16)
    kr128 = (proj[:, _N_KR128:_N_KR128 + 128] * cm_ref[...]
             + proj[:, _N_KR128P:_N_KR128P + 128] * sm_ref[...])
    knope = _dot(ckvb, wuk_ref[...])
    for h in range(MLA_HEADS):
        lo = h * HEAD_PAD
        km_o[:, lo:lo + HEAD_PAD] = (knope[:, lo:lo + HEAD_PAD] + kr128).astype(BF16)
    row = lax.broadcasted_iota(jnp.int32, (V_ROWS - MLA_V, tm), 0)
    ones_rows = jnp.where(row == 0, 1.0, 0.0).astype(BF16)
    vmt = _dot_nt(wuvt_ref[...], ckvb)
    projt = _dot_nt(wt_ref[...], xn)
    for h in range(MLA_HEADS):
        vmt_o[h, 0, 0:MLA_V, :] = vmt[h * MLA_V:(h + 1) * MLA_V, :].astype(BF16)
        vmt_o[h, 0, MLA_V:V_ROWS, :] = ones_rows
        vdt_o[h, 0, 0:DIFF_V, :] = projt[1536 + h * DIFF_V:1536 + (h + 1) * DIFF_V, :].astype(BF16)
        vdt_o[h, 0, DIFF_V:V_ROWS, :] = ones_rows
    qmt = _dot_nt(wuqt_ref[...], cqn)
    qmpt = _dot_nt(wuqpt_ref[...], cqn)
    cmt = cmt_ref[...]
    smt = smt_ref[...]
    for h in range(MLA_HEADS):
        lo = h * HEAD_PAD
        qmt_o[lo:lo + HEAD_PAD, :] = (qmt[lo:lo + HEAD_PAD, :] * cmt
                                      + qmpt[lo:lo + HEAD_PAD, :] * smt).astype(BF16)
    cdt = cdt_ref[...]
    sdt = sdt_ref[...]
    for j in range(4):
        lo = j * 128
        qdt_o[lo:lo + 128, :] = (projt[lo:lo + 128, :] * cdt
                                 + projt[512 + lo:512 + lo + 128, :] * sdt).astype(BF16)
    qct_o[...] = (projt[1024:1536, :] * MEM_SCALE).astype(BF16)


def _pre_sample_kernel(x_ref, g_ref, wn_ref, gq_ref, gkv_ref, c32_ref, s32_ref, cd_ref, sd_ref,
                       cm_ref, sm_ref, wuq_ref, wuqp_ref,
                       ckv_o, kr_o, dk_o, dv_o, qm_o, dq_o, mq_o):
    xn, proj, cqn, ckvb, dk_tiles = _pre_common(
        x_ref, g_ref, wn_ref, gq_ref, gkv_ref, c32_ref, s32_ref, cd_ref, sd_ref,
        ckv_o, kr_o, dk_o, dv_o)
    qm = _dot(cqn, wuq_ref[...])
    qmp = _dot(cqn, wuqp_ref[...])
    cm = cm_ref[...] * MLA_SCALE
    sm = sm_ref[...] * MLA_SCALE
    for h in range(MLA_HEADS):
        lo = h * HEAD_PAD
        qm_o[:, lo:lo + HEAD_PAD] = qm[:, lo:lo + HEAD_PAD] * cm + qmp[:, lo:lo + HEAD_PAD] * sm
    cd = cd_ref[...] * DIFF_SCALE
    sd = sd_ref[...] * DIFF_SCALE
    for j in range(4):
        lo = j * 128
        dq_o[:, lo:lo + 128] = (proj[:, _N_DQ + lo:_N_DQ + lo + 128] * cd
                                + proj[:, _N_DQP + lo:_N_DQP + lo + 128] * sd)
    mq_o[...] = proj[:, _N_MQ:_N_MQ + 512] * MEM_SCALE


def _row_spec(tm, w):
    return pl.BlockSpec((tm, w), lambda i: (i, 0))


def _col_spec(h, tm):
    return pl.BlockSpec((h, tm), lambda i: (0, i))


def _pre_prompt(x, p, tabs):
    rows = x.shape[0]
    tm = TK
    nkb = rows // tm
    in_specs = [
        _row_spec(tm, D_MODEL), _full((1, D_MODEL)), _full(p['wn_prompt'].shape),
        _full((1, MLA_Q_RANK)), _full((1, MLA_KV_RANK)),
        _row_spec(tm, 32), _row_spec(tm, 32), _row_spec(tm, 128), _row_spec(tm, 128),
        _row_spec(tm, 128), _row_spec(tm, 128),
        _col_spec(128, tm), _col_spec(128, tm), _col_spec(128, tm), _col_spec(128, tm),
        _full(p['wt'].shape), _full(p['wuqt'].shape), _full(p['wuqpt'].shape),
        _full(p['wuk_pad'].shape), _full(p['wuvt'].shape),
    ]
    vt_spec = pl.BlockSpec((8, 1, V_ROWS, tm), lambda i: (0, i, 0, 0))
    out_specs = [
        _row_spec(tm, 256), _row_spec(tm, 32), _row_spec(tm, 512), _row_spec(tm, 512),
        _row_spec(tm, 512), _row_spec(tm, 1024), vt_spec,
        _col_spec(1024, tm), _col_spec(512, tm), _col_spec(512, tm), vt_spec,
    ]
    out_shape = [
        jax.ShapeDtypeStruct((rows, 256), F32), jax.ShapeDtypeStruct((rows, 32), F32),
        jax.ShapeDtypeStruct((rows, 512), F32), jax.ShapeDtypeStruct((rows, 512), F32),
        jax.ShapeDtypeStruct((rows, 512), BF16), jax.ShapeDtypeStruct((rows, 1024), BF16),
        jax.ShapeDtypeStruct((8, nkb, V_ROWS, tm), BF16),
        jax.ShapeDtypeStruct((1024, rows), BF16), jax.ShapeDtypeStruct((512, rows), BF16),
        jax.ShapeDtypeStruct((512, rows), BF16),
        jax.ShapeDtypeStruct((8, nkb, V_ROWS, tm), BF16),
    ]
    return pl.pallas_call(
        _pre_prompt_kernel,
        grid=(rows // tm,),
        in_specs=in_specs, out_specs=out_specs, out_shape=out_shape,
        compiler_params=pltpu.CompilerParams(dimension_semantics=("parallel",),
                                             vmem_limit_bytes=VMEM_LIMIT),
        name="pre_prompt",
    )(x, p['pre_mix_g'], p['wn_prompt'], p['gq'], p['gkv'],
      tabs['c32'], tabs['s32'], tabs['cd'], tabs['sd'], tabs['cm'], tabs['sm'],
      tabs['cmt'], tabs['smt'], tabs['cdt'], tabs['sdt'],
      p['wt'], p['wuqt'], p['wuqpt'], p['wuk_pad'], p['wuvt'])


def _pre_sample(x, p, tabs):
    rows = x.shape[0]
    tm = 256
    in_specs = [
        _row_spec(tm, D_MODEL), _full((1, D_MODEL)), _full(p['wn_sample'].shape),
        _full((1, MLA_Q_RANK)), _full((1, MLA_KV_RANK)),
        _row_spec(tm, 32), _row_spec(tm, 32), _row_spec(tm, 128), _row_spec(tm, 128),
        _row_spec(tm, 128), _row_spec(tm, 128),
        _full(p['wuq_pad'].shape), _full(p['wuqp_pad'].shape),
    ]
    out_specs = [
        _row_spec(tm, 256), _row_spec(tm, 32), _row_spec(tm, 512), _row_spec(tm, 512),
        _row_spec(tm, 1024), _row_spec(tm, 512), _row_spec(tm, 512),
    ]
    out_shape = [
        jax.ShapeDtypeStruct((rows, 256), F32), jax.ShapeDtypeStruct((rows, 32), F32),
        jax.ShapeDtypeStruct((rows, 512), F32), jax.ShapeDtypeStruct((rows, 512), F32),
        jax.ShapeDtypeStruct((rows, 1024), F32), jax.ShapeDtypeStruct((rows, 512), F32),
        jax.ShapeDtypeStruct((rows, 512), F32),
    ]
    return pl.pallas_call(
        _pre_sample_kernel,
        grid=(rows // tm,),
        in_specs=in_specs, out_specs=out_specs, out_shape=out_shape,
        compiler_params=pltpu.CompilerParams(dimension_semantics=("parallel",),
                                             vmem_limit_bytes=VMEM_LIMIT),
        name="pre_sample",
    )(x, p['pre_mix_g'], p['wn_sample'], p['gq'], p['gkv'],
      tabs['c32'], tabs['s32'], tabs['cd'], tabs['sd'], tabs['cm'], tabs['sm'],
      p['wuq_pad'], p['wuqp_pad'])


def _chunk_mask(i, j, reps):
    kc = (j * TK + lax.broadcasted_iota(jnp.int32, (TK, TQ), 0)) // CHUNK
    qc = (i * TQ + lax.broadcasted_iota(jnp.int32, (TK, TQ), 1)) // CHUNK
    mask = kc <= qc
    return mask if reps == 1 else jnp.concatenate([mask] * reps, axis=1)


def _flash_blocks(i, heads, pv_lag, qk, softmax, pv):
    def run(steps, masked):
        pending = []
        for j_next, slot_next, j, slot in steps:
            for h in range(heads):
                if j_next is not None:
                    qk(j_next, slot_next, h)
                pending.append((j, h) + softmax(j, slot, h, masked))
                if len(pending) > pv_lag:
                    pv(*pending.pop(0))
        for item in pending:
            pv(*item)

    for h in range(heads):
        qk(0, 0, h)

    def pair(j):
        return [(j + 1, 1, j, 0), (j + 2, 0, j + 1, 1)]

    def body4(jj, c):
        run(pair(4 * jj) + pair(4 * jj + 2), False)
        return c

    def body2(jj, c):
        run(pair(2 * (i - 1)), False)
        return c

    lax.fori_loop(0, i // 2, body4, 0)
    lax.fori_loop(0, i % 2, body2, 0)
    run([(2 * i + 1, 1, 2 * i, 0), (None, None, 2 * i + 1, 1)], True)


def _online_softmax(st, h, m_ref):
    m_old = m_ref[h]
    m_new = jnp.maximum(m_old, jnp.max(st, axis=0, keepdims=True))
    m_ref[h] = m_new
    return jnp.exp2(m_old - m_new), jnp.exp2(st - m_new).astype(BF16)


def _online_pv(j, h, alpha, p, acc_ref, vt_ref):
    acc_ref[h] = alpha * acc_ref[h] + _dot(vt_ref[h, j], p)


def _kblock(j):
    return pl.ds(pl.multiple_of(j * TK, TK), TK)


def _attn_mla_kernel(qt_ref, k_ref, vt_ref, o_ref, s_ref, m_ref, acc_ref):
    i = pl.program_id(1)
    heads = qt_ref.shape[0] // HEAD_PAD
    m_ref[...] = jnp.full(m_ref.shape, NEG_INF, F32)
    acc_ref[...] = jnp.zeros(acc_ref.shape, F32)

    def qk(j, slot, h):
        hs = slice(h * HEAD_PAD, (h + 1) * HEAD_PAD)
        s_ref[slot, h] = _dot(k_ref[_kblock(j), hs], qt_ref[hs, :])

    def softmax(j, slot, h, masked):
        st = s_ref[slot, h]
        if masked:
            st = jnp.where(_chunk_mask(i, j, 1), st, NEG_INF)
        return _online_softmax(st, h, m_ref)

    def pv(j, h, alpha, p):
        _online_pv(j, h, alpha, p, acc_ref, vt_ref)

    _flash_blocks(i, heads, MLA_PV_LAG, qk, softmax, pv)
    for h in range(heads):
        acc = acc_ref[h]
        o_ref[h * MLA_V:(h + 1) * MLA_V, :] = (
            acc[0:MLA_V, :] / acc[MLA_V:MLA_V + 1, :]).astype(BF16)


def _attn_mla(qmt, km, vmt):
    t = km.shape[0]
    hg = ATTN_HEADS_PER_STEP
    groups = MLA_HEADS // hg
    nq = t // TQ
    resident = pl.Buffered(1)
    return pl.pallas_call(
        _attn_mla_kernel,
        grid=(groups, nq),
        in_specs=[
            pl.BlockSpec((hg * HEAD_PAD, TQ), lambda g, i: (g, i)),
            pl.BlockSpec((t, hg * HEAD_PAD), lambda g, i: (0, g), pipeline_mode=resident),
            pl.BlockSpec((hg, t // TK, V_ROWS, TK), lambda g, i: (g, 0, 0, 0),
                         pipeline_mode=resident),
        ],
        out_specs=pl.BlockSpec((hg * MLA_V, TQ), lambda g, i: (g, i)),
        out_shape=jax.ShapeDtypeStruct((MLA_HEADS * MLA_V, t), BF16),
        scratch_shapes=[pltpu.VMEM((2, hg, TK, TQ), F32), pltpu.VMEM((hg, 1, TQ), F32),
                        pltpu.VMEM((hg, V_ROWS, TQ), F32)],
        compiler_params=pltpu.CompilerParams(dimension_semantics=("parallel", "parallel"),
                                             vmem_limit_bytes=VMEM_LIMIT),
        name="attn_mla",
    )(qmt, km, vmt)


def _attn_diff_kernel(lam_init, qt_ref, k_ref, vt_ref, lq1_ref, lk1_ref, lq2_ref, lk2_ref,
                      gsub_ref, o_ref, wq_ref, s_ref, m_ref, acc_ref):
    i = pl.program_id(1)
    heads = vt_ref.shape[0]
    grp = lax.broadcasted_iota(jnp.int32, (128, TQ), 0) // DIFF_DC
    for pr in range(heads // 2):
        qt = qt_ref[pr * 128:(pr + 1) * 128, :]
        zero = jnp.zeros_like(qt)
        for s in range(4):
            wq_ref[pr, :, s * TQ:(s + 1) * TQ] = jnp.where(grp == s, qt, zero)
    m_ref[...] = jnp.full(m_ref.shape, NEG_INF, F32)
    acc_ref[...] = jnp.zeros(acc_ref.shape, F32)

    def qk(j, slot, h):
        pr, hh = divmod(h, 2)
        k = k_ref[_kblock(j), pr * 128:(pr + 1) * 128]
        s_ref[slot, h] = _dot(k, wq_ref[pr, :, hh * 2 * TQ:(hh + 1) * 2 * TQ])

    def softmax(j, slot, h, masked):
        st = s_ref[slot, h]
        if masked:
            st = jnp.where(_chunk_mask(i, j, 2), st, NEG_INF)
        return _online_softmax(st, h, m_ref)

    def pv(j, h, alpha, p):
        _online_pv(j, h, alpha, p, acc_ref, vt_ref)

    _flash_blocks(i, heads, DIFF_PV_LAG, qk, softmax, pv)

    lam = (jnp.exp(jnp.sum(lq1_ref[...] * lk1_ref[...], axis=-1, keepdims=True))
           - jnp.exp(jnp.sum(lq2_ref[...] * lk2_ref[...], axis=-1, keepdims=True))
           + lam_init)
    g = gsub_ref[...] * (1.0 - lam_init)
    for h in range(heads):
        acc = acc_ref[h]
        o0 = acc[0:DIFF_V, 0:TQ] / acc[DIFF_V:DIFF_V + 1, 0:TQ]
        o1 = acc[0:DIFF_V, TQ:2 * TQ] / acc[DIFF_V:DIFF_V + 1, TQ:2 * TQ]
        o = o0 - lam * o1
        o = o * lax.rsqrt(jnp.mean(o * o, axis=0, keepdims=True) + EPS) * g
        o_ref[h * DIFF_V:(h + 1) * DIFF_V, :] = o.astype(BF16)


def _attn_diff(qdt, kd, vdt, lq1, lk1, lq2, lk2, gsub, lam_init):
    t = kd.shape[0]
    nq = t // TQ
    hg = ATTN_HEADS_PER_STEP
    groups = DIFF_HEADS // hg
    vec = _full((1, DIFF_DC))
    resident = pl.Buffered(1)
    return pl.pallas_call(
        functools.partial(_attn_diff_kernel, lam_init),
        grid=(groups, nq),
        in_specs=[
            pl.BlockSpec((hg * DIFF_V, TQ), lambda g, i: (g, i)),
            pl.BlockSpec((t, hg * DIFF_V), lambda g, i: (0, g), pipeline_mode=resident),
            pl.BlockSpec((hg, t // TK, V_ROWS, TK), lambda g, i: (g, 0, 0, 0),
                         pipeline_mode=resident),
            vec, vec, vec, vec, _full((DIFF_V, 1)),
        ],
        out_specs=pl.BlockSpec((hg * DIFF_V, TQ), lambda g, i: (g, i)),
        out_shape=jax.ShapeDtypeStruct((DIFF_HEADS * DIFF_V, t), BF16),
        scratch_shapes=[pltpu.VMEM((hg // 2, 128, 4 * TQ), BF16),
                        pltpu.VMEM((2, hg, TK, 2 * TQ), F32),
                        pltpu.VMEM((hg, 1, 2 * TQ), F32),
                        pltpu.VMEM((hg, V_ROWS, 2 * TQ), F32)],
        compiler_params=pltpu.CompilerParams(dimension_semantics=("parallel", "parallel"),
                                             vmem_limit_bytes=VMEM_LIMIT),
        name="attn_diff",
    )(qdt, kd, vdt, lq1, lk1, lq2, lk2, gsub)


def _attn_mem_kernel(qt_ref, k_ref, vt_ref, o_ref):
    for h in range(MEM_HEADS):
        sl = slice(h * MEM_DH, (h + 1) * MEM_DH)
        st = _dot(k_ref[:, sl], qt_ref[sl, :])
        e = jnp.exp(st - jnp.max(st, axis=0, keepdims=True))
        p = (e / jnp.sum(e, axis=0, keepdims=True)).astype(BF16)
        o_ref[sl, :] = _dot(vt_ref[sl, :], p).astype(BF16)


def _attn_mem(qct, mk, mvt):
    w, t = qct.shape
    tq = 512
    return pl.pallas_call(
        _attn_mem_kernel,
        grid=(t // tq,),
        in_specs=[pl.BlockSpec((w, tq), lambda i: (0, i)), _full(mk.shape), _full(mvt.shape)],
        out_specs=pl.BlockSpec((w, tq), lambda i: (0, i)),
        out_shape=jax.ShapeDtypeStruct((w, t), BF16),
        compiler_params=pltpu.CompilerParams(dimension_semantics=("parallel",)),
        name="attn_mem",
    )(qct, mk, mvt)


def _softmax_two(s_a, s_b):
    m = jnp.maximum(jnp.max(s_a, axis=-1, keepdims=True), jnp.max(s_b, axis=-1, keepdims=True))
    e_a = jnp.exp(s_a - m)
    e_b = jnp.exp(s_b - m)
    inv = 1.0 / (jnp.sum(e_a, axis=-1, keepdims=True) + jnp.sum(e_b, axis=-1, keepdims=True))
    return e_a * inv, e_b * inv


def _diag_blocks(o_all, rows, width, nblk):
    lane_blk = lax.broadcasted_iota(jnp.int32, (rows, nblk * width), 1) // width
    out = jnp.zeros((rows, nblk * width), F32)
    for b in range(nblk):
        out = out + jnp.where(lane_blk == b, o_all[b * rows:(b + 1) * rows, :], 0.0)
    return out


def _sample_mla_kernel(qm_ref, ckvn_ref, krn_ref, mq_ref, ckvp_ref, krp_ref, mk_ref, mv_ref,
                       wabs_ref, wuv_ref, oa_ref, oc_ref):
    nq = qm_ref.shape[0]
    qm = qm_ref[...].astype(BF16)
    qext = jnp.concatenate(
        [_dot(qm[:, h * HEAD_PAD:(h + 1) * HEAD_PAD], wabs_ref[h]) for h in range(MLA_HEADS)],
        axis=0).astype(BF16)
    q_lat = qext[:, 0:MLA_KV_RANK]
    q_rope = qext[:, MLA_KV_RANK:MLA_KV_RANK + MLA_ROPE]
    ckv_p = ckvp_ref[...].astype(BF16)
    ckv_n = ckvn_ref[...].astype(BF16)
    s_p = _dot_nt(q_lat, ckv_p) + _dot_nt(q_rope, krp_ref[...].astype(BF16))
    s_n = _dot_nt(q_lat, ckv_n) + _dot_nt(q_rope, krn_ref[...].astype(BF16))
    p_p, p_n = _softmax_two(s_p, s_n)
    o_lat = _dot(p_p.astype(BF16), ckv_p) + _dot(p_n.astype(BF16), ckv_n)
    o_all = _dot(o_lat.astype(BF16), wuv_ref[...])
    oa_ref[...] = _diag_blocks(o_all, nq, MLA_V, MLA_HEADS)
    mq = mq_ref[...].astype(BF16)
    for h in range(MEM_HEADS):
        sl = slice(h * MEM_DH, (h + 1) * MEM_DH)
        s = _dot_nt(mq[:, sl], mk_ref[:, sl].astype(BF16))
        e = jnp.exp(s - jnp.max(s, axis=-1, keepdims=True))
        p = (e / jnp.sum(e, axis=-1, keepdims=True)).astype(BF16)
        oc_ref[:, sl] = _dot(p, mv_ref[:, sl].astype(BF16))


def _sample_mla(qm, ckv_new, kr_new, mq, ckv_past, kr_past, mem_k, mem_v, wabs, wuv, nb, nq):
    def rows(w):
        return pl.BlockSpec((nq, w), lambda b: (b, 0))

    def cache(shape):
        return pl.BlockSpec((None,) + shape, lambda b: (b, 0, 0))

    past = ckv_past.shape[1]
    return pl.pallas_call(
        _sample_mla_kernel,
        grid=(nb,),
        in_specs=[rows(1024), rows(256), rows(32), rows(512),
                  cache((past, 256)), cache((past, 32)), cache(mem_k.shape[1:]),
                  cache(mem_v.shape[1:]), _full(wabs.shape), _full(wuv.shape)],
        out_specs=[rows(512), rows(512)],
        out_shape=[jax.ShapeDtypeStruct((nb * nq, 512), F32)] * 2,
        compiler_params=pltpu.CompilerParams(dimension_semantics=("parallel",),
                                             vmem_limit_bytes=VMEM_LIMIT),
        name="sample_mla_mem",
    )(qm, ckv_new, kr_new, mq, ckv_past, kr_past, mem_k, mem_v, wabs, wuv)


def _sample_diff_kernel(lam_init, dq_ref, dkn_ref, dvn_ref, dkp_ref, dvp_ref,
                        lq1_ref, lk1_ref, lq2_ref, lk2_ref, o_ref):
    nq = dq_ref.shape[0]
    w = dq_ref.shape[1]
    nmaps = w // DIFF_DC
    q = dq_ref[...]
    row_grp = lax.broadcasted_iota(jnp.int32, (nmaps * nq, w), 0) // nq
    lane_grp = lax.broadcasted_iota(jnp.int32, (nmaps * nq, w), 1) // DIFF_DC
    qbd = jnp.where(row_grp == lane_grp, jnp.concatenate([q] * nmaps, axis=0), 0.0).astype(BF16)
    k_p = dkp_ref[...].astype(BF16)
    k_n = dkn_ref[...].astype(BF16)
    p_p, p_n = _softmax_two(_dot_nt(qbd, k_p), _dot_nt(qbd, k_n))
    lam = (jnp.exp(jnp.sum(lq1_ref[...] * lk1_ref[...], axis=-1, keepdims=True))
           - jnp.exp(jnp.sum(lq2_ref[...] * lk2_ref[...], axis=-1, keepdims=True))
           + lam_init)

    def combine(p):
        parts = [p[(2 * h) * nq:(2 * h + 1) * nq, :] - lam * p[(2 * h + 1) * nq:(2 * h + 2) * nq, :]
                 for h in range(nmaps // 2)]
        return jnp.concatenate(parts, axis=0).astype(BF16)

    o_all = (_dot(combine(p_p), dvp_ref[...].astype(BF16))
             + _dot(combine(p_n), dvn_ref[...].astype(BF16)))
    o_ref[...] = _diag_blocks(o_all, nq, DIFF_V, nmaps // 2)


def _sample_diff(dq, dk_new, dv_new, dk_past, dv_past, lq1, lk1, lq2, lk2, lam_init, nb, nq):
    hw = 256
    past = dk_past.shape[1]
    rows = pl.BlockSpec((nq, hw), lambda b, g: (b, g))
    cache = pl.BlockSpec((None, past, hw), lambda b, g: (b, 0, g))
    vec = _full((1, DIFF_DC))
    return pl.pallas_call(
        functools.partial(_sample_diff_kernel, lam_init),
        grid=(nb, 512 // hw),
        in_specs=[rows, rows, rows, cache, cache, vec, vec, vec, vec],
        out_specs=rows,
        out_shape=jax.ShapeDtypeStruct((nb * nq, 512), F32),
        compiler_params=pltpu.CompilerParams(dimension_semantics=("parallel", "parallel"),
                                             vmem_limit_bytes=VMEM_LIMIT),
        name="sample_diff",
    )(dq, dk_new, dv_new, dk_past, dv_past, lq1, lk1, lq2, lk2)


def _subln_kernel(lam_init, o_ref, g_ref, out_ref):
    g = g_ref[...] * (1.0 - lam_init)
    for h in range(DIFF_HEADS):
        o = o_ref[h * DIFF_V:(h + 1) * DIFF_V, :]
        out_ref[h * DIFF_V:(h + 1) * DIFF_V, :] = (
            o * lax.rsqrt(jnp.mean(o * o, axis=0, keepdims=True) + EPS) * g).astype(BF16)


def _subln(ot, gsub, lam_init):
    return pl.pallas_call(
        functools.partial(_subln_kernel, lam_init),
        out_shape=jax.ShapeDtypeStruct(ot.shape, BF16),
        name="sample_subln",
    )(ot, gsub)


def _mix_kernel(x_ref, oa_ref, ob_ref, oc_ref, g_ref, wg_ref, bg_ref, woa_ref, wob_ref, woc_ref,
                wout_ref, gpost_ref, y_ref):
    x = x_ref[...]
    xn = _rms(x, g_ref[...]).astype(BF16)
    d = x.shape[1]
    merged = None
    for b, (o_ref, w_ref) in enumerate(((oa_ref, woa_ref), (ob_ref, wob_ref), (oc_ref, woc_ref))):
        gate = jax.nn.sigmoid(_dot(xn, wg_ref[:, b * d:(b + 1) * d]) + bg_ref[:, b * d:(b + 1) * d])
        term = gate * _dot_tn(o_ref[...], w_ref[...])
        merged = term if merged is None else merged + term
    mix = _dot(merged.astype(BF16), wout_ref[...])
    y_ref[...] = x + _rms(mix, gpost_ref[...])


def _mix(x, oat, obt, oct, p):
    rows, d = x.shape
    tm = 512 if rows % 512 == 0 else rows
    ot_spec = pl.BlockSpec((512, tm), lambda i: (0, i))
    return pl.pallas_call(
        _mix_kernel,
        grid=(rows // tm,),
        in_specs=[_row_spec(tm, d), ot_spec, ot_spec, ot_spec, _full((1, d)),
                  _full(p['w_gate'].shape), _full((1, 3 * d)), _full((512, d)), _full((512, d)),
                  _full((512, d)), _full((d, d)), _full((1, d))],
        out_specs=_row_spec(tm, d),
        out_shape=jax.ShapeDtypeStruct((rows, d), F32),
        compiler_params=pltpu.CompilerParams(dimension_semantics=("parallel",),
                                             vmem_limit_bytes=VMEM_LIMIT),
        name="mix",
    )(x, oat, obt, oct, p['pre_mix_g'], p['w_gate'], p['b_gate'], p['w_o_mla'], p['w_o_diff'],
      p['w_o_mem'], p['w_out'], p['post_mix_g'])


def _mlp_kernel(x_ref, g_ref, wup_ref, wdn_ref, gpost_ref, y_ref):
    x = x_ref[...]
    h = _rms(x, g_ref[...]).astype(BF16)
    u = jnp.maximum(_dot(h, wup_ref[...]), 0.0)
    f = _dot((u * u).astype(BF16), wdn_ref[...])
    y_ref[...] = x + _rms(f, gpost_ref[...])


def _mlp(x, p):
    rows, d = x.shape
    tm = 256
    return pl.pallas_call(
        _mlp_kernel,
        grid=(rows // tm,),
        in_specs=[_row_spec(tm, d), _full((1, d)), _full(p['w_mlp_up'].shape),
                  _full(p['w_mlp_down'].shape), _full((1, d))],
        out_specs=_row_spec(tm, d),
        out_shape=jax.ShapeDtypeStruct((rows, d), F32),
        compiler_params=pltpu.CompilerParams(dimension_semantics=("parallel",),
                                             vmem_limit_bytes=VMEM_LIMIT),
        name="mlp",
    )(x, p['pre_mlp_g'], p['w_mlp_up'], p['w_mlp_down'], p['post_mlp_g'])


def _partner(width, group, half):
    idx = np.arange(width)
    sign = np.zeros(width, np.float32)
    d = idx % group
    first = d < half
    second = (d >= half) & (d < 2 * half)
    src = np.where(first, idx + half, np.where(second, idx - half, idx))
    sign[first] = -1.0
    sign[second] = 1.0
    return src, sign


def _take_signed(w, src, sign):
    return w[:, src] * jnp.asarray(sign)[None, :]


def _pad_cols(w, total):
    return jnp.pad(w, ((0, 0), (0, total - w.shape[1])))


def _prep_layer(l, w_in, mla_w_uq, mla_w_uk, mla_w_uv, w_mem_k, w_mem_v, w_o_mla, w_o_diff, w_o_mem,
                w_gate, b_gate, w_out, w_mlp_up, w_mlp_down, gains):
    cq, ckv, kr, dq, dk, dv, mq = jnp.split(
        w_in[l], np.cumsum((384, 256, 32, 512, 512, 512))[...].tolist(), axis=1)
    src_d, sign_d = _partner(512, DIFF_DC, DIFF_ROT // 2)
    src_r, sign_r = _partner(32, 32, MLA_ROPE // 2)
    dkp = _take_signed(dk, src_d, sign_d)
    dqp = _take_signed(dq, src_d, sign_d)
    krp = _take_signed(kr, src_r, sign_r)
    common = [cq, ckv, dk, dkp, dv, _pad_cols(kr, 128), _pad_cols(krp, 128)]
    zeros64 = jnp.zeros((D_MODEL, MLA_NOPE), F32)
    kr128 = _pad_cols(jnp.concatenate([zeros64, kr], axis=1), 128)
    kr128p = _pad_cols(jnp.concatenate([zeros64, krp], axis=1), 128)
    head_w = MLA_NOPE + MLA_ROPE
    e = np.arange(HEAD_PAD)
    valid = e < head_w
    src_q = np.concatenate([h * head_w + np.where(valid, e, 0) for h in range(MLA_HEADS)])
    sign_q = np.tile(valid.astype(np.float32), MLA_HEADS)
    in_rope1 = (e >= MLA_NOPE) & (e < MLA_NOPE + MLA_ROPE // 2)
    in_rope2 = (e >= MLA_NOPE + MLA_ROPE // 2) & valid
    pe = np.where(in_rope1, e + MLA_ROPE // 2, np.where(in_rope2, e - MLA_ROPE // 2, 0))
    src_qp = np.concatenate([h * head_w + pe for h in range(MLA_HEADS)])
    sign_qp = np.tile(np.where(in_rope1, -1.0, np.where(in_rope2, 1.0, 0.0)).astype(np.float32),
                      MLA_HEADS)
    wuq_pad = _take_signed(mla_w_uq[l], src_q, sign_q)
    wuqp_pad = _take_signed(mla_w_uq[l], src_qp, sign_qp)
    wuk = mla_w_uk[l]
    wuk_pad = jnp.pad(wuk, ((0, 0), (0, 0), (0, HEAD_PAD - MLA_NOPE))).reshape(MLA_KV_RANK, -1)
    wuv = mla_w_uv[l].reshape(MLA_KV_RANK, MLA_HEADS * MLA_V)
    sel = np.zeros((HEAD_PAD, 128), np.float32)
    sel[MLA_NOPE + np.arange(MLA_ROPE), np.arange(MLA_ROPE)] = 1.0
    wabs = jnp.concatenate([
        jnp.pad(jnp.transpose(wuk, (1, 2, 0)), ((0, 0), (0, HEAD_PAD - MLA_NOPE), (0, 0))),
        jnp.broadcast_to(jnp.asarray(sel), (MLA_HEADS, HEAD_PAD, 128))], axis=2)
    bf = lambda a: a.astype(BF16)
    row = lambda a: a[l][None, :]
    p = {
        'wn_prompt': bf(jnp.concatenate(common + [kr128, kr128p], axis=1)),
        'wn_sample': bf(jnp.concatenate(common + [dq, dqp, mq], axis=1)),
        'wt': bf(jnp.concatenate([dq, dqp, mq, dv], axis=1).T),
        'wuqt': bf(wuq_pad.T), 'wuqpt': bf(wuqp_pad.T),
        'wuq_pad': bf(wuq_pad), 'wuqp_pad': bf(wuqp_pad),
        'wuk_pad': bf(wuk_pad), 'wuvt': bf(wuv.T), 'wuv': bf(wuv), 'wabs': bf(wabs),
        'w_mem_k': bf(w_mem_k[l]), 'w_mem_v': bf(w_mem_v[l]), 'w_mem_vt': bf(w_mem_v[l].T),
        'w_o_mla': bf(w_o_mla[l]), 'w_o_diff': bf(w_o_diff[l]), 'w_o_mem': bf(w_o_mem[l]),
        'w_gate': bf(w_gate[l]), 'b_gate': row(b_gate), 'w_out': bf(w_out[l]),
        'w_mlp_up': bf(w_mlp_up[l]), 'w_mlp_down': bf(w_mlp_down[l]),
    }
    for name, g in gains.items():
        p[name] = row(g)
    return p


def _rope_cos_sin(pos, rot_dim, theta):
    half = rot_dim // 2
    inv = jnp.power(jnp.float32(theta), -jnp.arange(half, dtype=F32) * (2.0 / rot_dim))
    ang = pos.astype(F32)[:, None] * inv[None, :]
    return jnp.cos(ang), jnp.sin(ang)


def _tables(pos, reps):
    n = pos.shape[0]
    cm, sm = _rope_cos_sin(pos, MLA_ROPE, MLA_THETA)
    cd, sd = _rope_cos_sin(pos, DIFF_ROT, ROPE_THETA)
    one = lambda w: jnp.ones((n, w), F32)
    zero = lambda w: jnp.zeros((n, w), F32)
    t = {
        'c32': jnp.concatenate([cm, cm], axis=1), 's32': jnp.concatenate([sm, sm], axis=1),
        'cm': jnp.concatenate([one(64), cm, cm, one(32)], axis=1),
        'sm': jnp.concatenate([zero(64), sm, sm, zero(32)], axis=1),
        'cd': jnp.tile(jnp.concatenate([cd, cd, one(24)], axis=1), (1, 4)),
        'sd': jnp.tile(jnp.concatenate([sd, sd, zero(24)], axis=1), (1, 4)),
    }
    return {k: jnp.tile(v, (reps, 1)) for k, v in t.items()}


def kernel(x_prompt, x_sample, cache_mla_ckv, cache_mla_krope, cache_diff_k, cache_diff_v, cache_mem_k, cache_mem_v, mem_prompt, pre_mix_g, w_in, mla_q_norm_g, mla_w_uq, mla_kv_norm_g, mla_w_uk, mla_w_uv, diff_lq1, diff_lk1, diff_lq2, diff_lk2, diff_subln_g, mem_norm_g, w_mem_k, w_mem_v, w_o_mla, w_o_diff, w_o_mem, w_gate, b_gate, w_out, post_mix_g, pre_mlp_g, w_mlp_up, w_mlp_down, post_mlp_g):
    depth = w_in.shape[0]
    bp, t, d = x_prompt.shape
    nb, nq, _ = x_sample.shape
    past = cache_mla_ckv.shape[2]
    assert bp == 1 and t % TQ == 0 and TQ == 2 * TK and TK % CHUNK == 0 and d == D_MODEL
    assert past % CHUNK == 0 and nq <= CHUNK

    tabs_p = _tables(jnp.arange(t, dtype=jnp.int32), 1)
    tabs_p['cmt'] = (tabs_p['cm'] * (MLA_SCALE * LOG2E)).T
    tabs_p['smt'] = (tabs_p['sm'] * (MLA_SCALE * LOG2E)).T
    tabs_p['cdt'] = (tabs_p['cd'] * (DIFF_SCALE * LOG2E)).T
    tabs_p['sdt'] = (tabs_p['sd'] * (DIFF_SCALE * LOG2E)).T
    tabs_s = _tables(past + jnp.arange(nq, dtype=jnp.int32), nb)

    xp = x_prompt.reshape(t, d)
    xs = x_sample.reshape(nb * nq, d)
    outs = {k: [] for k in ('p_ckv', 'p_kr', 'p_dk', 'p_dv', 'p_mk', 'p_mv',
                            's_ckv', 's_kr', 's_dk', 's_dv')}
    for l in range(depth):
        lam_init = 0.8 - 0.6 * math.exp(-0.3 * l)
        p = _prep_layer(l, w_in, mla_w_uq, mla_w_uk, mla_w_uv, w_mem_k, w_mem_v, w_o_mla, w_o_diff,
                        w_o_mem, w_gate, b_gate, w_out, w_mlp_up, w_mlp_down,
                        {'pre_mix_g': pre_mix_g, 'gq': mla_q_norm_g, 'gkv': mla_kv_norm_g,
                         'mem_norm_g': mem_norm_g, 'post_mix_g': post_mix_g,
                         'pre_mlp_g': pre_mlp_g, 'post_mlp_g': post_mlp_g})
        lq1, lk1, lq2, lk2 = (a[l][None, :] for a in (diff_lq1, diff_lk1, diff_lq2, diff_lk2))
        gsub = diff_subln_g[l][:, None]

        mk, mv, mkb, mvt = _memkv(mem_prompt[0], p['mem_norm_g'], p['w_mem_k'], p['w_mem_v'],
                                  p['w_mem_vt'])
        (ckv_p, kr_p, dk_p, dv_p, kd, km, vmt, qmt, qdt, qct, vdt) = _pre_prompt(xp, p, tabs_p)
        oat = _attn_mla(qmt, km, vmt)
        obt = _attn_diff(qdt, kd, vdt, lq1, lk1, lq2, lk2, gsub, lam_init)
        oct = _attn_mem(qct, mkb, mvt)
        xp = _mlp(_mix(xp, oat, obt, oct, p), p)

        (ckv_s, kr_s, dk_s, dv_s, qm_s, dq_s, mq_s) = _pre_sample(xs, p, tabs_s)
        oa_s, oc_s = _sample_mla(qm_s, ckv_s, kr_s, mq_s, cache_mla_ckv[l], cache_mla_krope[l],
                                 cache_mem_k[l].reshape(nb, -1, 512),
                                 cache_mem_v[l].reshape(nb, -1, 512), p['wabs'], p['wuv'], nb, nq)
        ob_s = _sample_diff(dq_s, dk_s, dv_s, cache_diff_k[l].reshape(nb, past, 512),
                            cache_diff_v[l].reshape(nb, past, 512), lq1, lk1, lq2, lk2,
                            lam_init, nb, nq)
        obt_s = _subln(ob_s.T, gsub, lam_init)
        xs = _mlp(_mix(xs, oa_s.T.astype(BF16), obt_s, oc_s.T.astype(BF16), p), p)

        outs['p_ckv'].append(ckv_p.reshape(1, t, MLA_KV_RANK))
        outs['p_kr'].append(kr_p.reshape(1, t, MLA_ROPE))
        outs['p_dk'].append(dk_p.reshape(1, t, DIFF_HEADS, DIFF_V))
        outs['p_dv'].append(dv_p.reshape(1, t, DIFF_HEADS, DIFF_V))
        outs['p_mk'].append(mk.reshape(1, -1, MEM_HEADS, MEM_DH))
        outs['p_mv'].append(mv.reshape(1, -1, MEM_HEADS, MEM_DH))
        outs['s_ckv'].append(ckv_s.reshape(nb, nq, MLA_KV_RANK))
        outs['s_kr'].append(kr_s.reshape(nb, nq, MLA_ROPE))
        outs['s_dk'].append(dk_s.reshape(nb, nq, DIFF_HEADS, DIFF_V))
        outs['s_dv'].append(dv_s.reshape(nb, nq, DIFF_HEADS, DIFF_V))

    st = lambda k: jnp.stack(outs[k], axis=0)
    return (xp.reshape(1, t, d), xs.reshape(nb, nq, d),
            st('p_ckv'), st('p_kr'), st('p_dk'), st('p_dv'), st('p_mk'), st('p_mv'),
            st('s_ckv'), st('s_kr'), st('s_dk'), st('s_dv'))
```

```python
import functools
import math

import numpy as np
import jax
import jax.numpy as jnp
from jax import lax
from jax.experimental import pallas as pl
from jax.experimental.pallas import tpu as pltpu

F32 = jnp.float32
BF16 = jnp.bfloat16

D_MODEL = 1024
CHUNK = 64
EPS = 1e-6
NEG_INF = -1e30
MLA_HEADS = 8
MLA_Q_RANK = 384
MLA_KV_RANK = 256
MLA_NOPE = 64
MLA_ROPE = 32
MLA_V = 64
MLA_THETA = 10000.0
MLA_SCALE = (MLA_NOPE + MLA_ROPE) ** -0.5
DIFF_HEADS = 8
DIFF_DC = 32
DIFF_V = 64
DIFF_ROT = 8
ROPE_THETA = 500000.0
DIFF_SCALE = DIFF_DC ** -0.5
MEM_HEADS = 4
MEM_DH = 128
MEM_SCALE = MEM_DH ** -0.5
LOG2E = math.log2(math.e)
HEAD_PAD = 128
V_ROWS = 80
TQ = 512
TK = 256
ATTN_HEADS_PER_STEP = 4
MLA_PV_LAG = 1
DIFF_PV_LAG = 0
VMEM_LIMIT = 56 * 1024 * 1024

_NT = (((1,), (1,)), ((), ()))
_TN = (((0,), (0,)), ((), ()))


def _dot(a, b):
    return jnp.dot(a, b, preferred_element_type=F32)


def _dot_nt(a, b):
    return lax.dot_general(a, b, _NT, preferred_element_type=F32)


def _dot_tn(a, b):
    return lax.dot_general(a, b, _TN, preferred_element_type=F32)


def _rms(x, g):
    return x * lax.rsqrt(jnp.mean(x * x, axis=-1, keepdims=True) + EPS) * g


def _full(shape):
    nd = len(shape)
    return pl.BlockSpec(shape, lambda *_: (0,) * nd)


def _memkv_kernel(mem_ref, g_ref, wk_ref, wv_ref, wvt_ref, k_ref, v_ref, kb_ref, vt_ref):
    mn = _rms(mem_ref[...], g_ref[...]).astype(BF16)
    k = _dot(mn, wk_ref[...])
    k_ref[...] = k
    kb_ref[...] = k.astype(BF16)
    v_ref[...] = _dot(mn, wv_ref[...])
    vt_ref[...] = _dot_nt(wvt_ref[...], mn).astype(BF16)


def _memkv(mem, g, wk, wv, wvt):
    n_mem = mem.shape[0]
    w = wk.shape[1]
    return pl.pallas_call(
        _memkv_kernel,
        out_shape=(jax.ShapeDtypeStruct((n_mem, w), F32), jax.ShapeDtypeStruct((n_mem, w), F32),
                   jax.ShapeDtypeStruct((n_mem, w), BF16), jax.ShapeDtypeStruct((w, n_mem), BF16)),
        name="memkv",
    )(mem, g, wk, wv, wvt)


_N_CQ, _N_CKV, _N_DK, _N_DKP, _N_DV, _N_KR, _N_KRP = 0, 384, 640, 1152, 1664, 2176, 2304
_N_COMMON = 2432
_N_KR128, _N_KR128P = 2432, 2560
_N_PROMPT = 2688
_N_DQ, _N_DQP, _N_MQ = 2432, 2944, 3456
_N_SAMPLE = 3968


def _pre_common(x_ref, g_ref, wn_ref, gq_ref, gkv_ref, c32_ref, s32_ref, cd_ref, sd_ref,
                ckv_o, kr_o, dk_o, dv_o):
    xn = _rms(x_ref[...], g_ref[...]).astype(BF16)
    proj = _dot(xn, wn_ref[...])
    cqn = _rms(proj[:, _N_CQ:_N_CQ + MLA_Q_RANK], gq_ref[...]).astype(BF16)
    ckvn = _rms(proj[:, _N_CKV:_N_CKV + MLA_KV_RANK], gkv_ref[...])
    ckv_o[...] = ckvn
    kr_o[...] = (proj[:, _N_KR:_N_KR + MLA_ROPE] * c32_ref[...]
                 + proj[:, _N_KRP:_N_KRP + MLA_ROPE] * s32_ref[...])
    cd = cd_ref[...]
    sd = sd_ref[...]
    dk_tiles = []
    for j in range(4):
        lo = j * 128
        dk_j = (proj[:, _N_DK + lo:_N_DK + lo + 128] * cd
                + proj[:, _N_DKP + lo:_N_DKP + lo + 128] * sd)
        dk_o[:, lo:lo + 128] = dk_j
        dk_tiles.append(dk_j)
    dv_o[...] = proj[:, _N_DV:_N_DV + 512]
    return xn, proj, cqn, ckvn.astype(BF16), dk_tiles


def _pre_prompt_kernel(x_ref, g_ref, wn_ref, gq_ref, gkv_ref, c32_ref, s32_ref, cd_ref, sd_ref,
                       cm_ref, sm_ref, cmt_ref, smt_ref, cdt_ref, sdt_ref,
                       wt_ref, wuqt_ref, wuqpt_ref, wuk_ref, wuvt_ref,
                       ckv_o, kr_o, dk_o, dv_o, kd_o, km_o, vmt_o, qmt_o, qdt_o, qct_o, vdt_o):
    xn, proj, cqn, ckvb, dk_tiles = _pre_common(
        x_ref, g_ref, wn_ref, gq_ref, gkv_ref, c32_ref, s32_ref, cd_ref, sd_ref,
        ckv_o, kr_o, dk_o, dv_o)
    tm = x_ref.shape[0]
    for j in range(4):
        kd_o[:, j * 128:(j + 1) * 128] = dk_tiles[j].astype(BF16)
    kr128 = (proj[:, _N_KR128:_N_KR128 + 128] * cm_ref[...]
             + proj[:, _N_KR128P:_N_KR128P + 128] * sm_ref[...])
    knope = _dot(ckvb, wuk_ref[...])
    for h in range(MLA_HEADS):
        lo = h * HEAD_PAD
        km_o[:, lo:lo + HEAD_PAD] = (knope[:, lo:lo + HEAD_PAD] + kr128).astype(BF16)
    row = lax.broadcasted_iota(jnp.int32, (V_ROWS - MLA_V, tm), 0)
    ones_rows = jnp.where(row == 0, 1.0, 0.0).astype(BF16)
    vmt = _dot_nt(wuvt_ref[...], ckvb)
    projt = _dot_nt(wt_ref[...], xn)
    for h in range(MLA_HEADS):
        vmt_o[h, 0, 0:MLA_V, :] = vmt[h * MLA_V:(h + 1) * MLA_V, :].astype(BF16)
        vmt_o[h, 0, MLA_V:V_ROWS, :] = ones_rows
        vdt_o[h, 0, 0:DIFF_V, :] = projt[1536 + h * DIFF_V:1536 + (h + 1) * DIFF_V, :].astype(BF16)
        vdt_o[h, 0, DIFF_V:V_ROWS, :] = ones_rows
    qmt = _dot_nt(wuqt_ref[...], cqn)
    qmpt = _dot_nt(wuqpt_ref[...], cqn)
    cmt = cmt_ref[...]
    smt = smt_ref[...]
    for h in range(MLA_HEADS):
        lo = h * HEAD_PAD
        qmt_o[lo:lo + HEAD_PAD, :] = (qmt[lo:lo + HEAD_PAD, :] * cmt
                                      + qmpt[lo:lo + HEAD_PAD, :] * smt).astype(BF16)
    cdt = cdt_ref[...]
    sdt = sdt_ref[...]
    for j in range(4):
        lo = j * 128
        qdt_o[lo:lo + 128, :] = (projt[lo:lo + 128, :] * cdt
                                 + projt[512 + lo:512 + lo + 128, :] * sdt).astype(BF16)
    qct_o[...] = (projt[1024:1536, :] * MEM_SCALE).astype(BF16)


def _pre_sample_kernel(x_ref, g_ref, wn_ref, gq_ref, gkv_ref, c32_ref, s32_ref, cd_ref, sd_ref,
                       cm_ref, sm_ref, wuq_ref, wuqp_ref,
                       ckv_o, kr_o, dk_o, dv_o, qm_o, dq_o, mq_o):
    xn, proj, cqn, ckvb, dk_tiles = _pre_common(
        x_ref, g_ref, wn_ref, gq_ref, gkv_ref, c32_ref, s32_ref, cd_ref, sd_ref,
        ckv_o, kr_o, dk_o, dv_o)
    qm = _dot(cqn, wuq_ref[...])
    qmp = _dot(cqn, wuqp_ref[...])
    cm = cm_ref[...] * MLA_SCALE
    sm = sm_ref[...] * MLA_SCALE
    for h in range(MLA_HEADS):
        lo = h * HEAD_PAD
        qm_o[:, lo:lo + HEAD_PAD] = qm[:, lo:lo + HEAD_PAD] * cm + qmp[:, lo:lo + HEAD_PAD] * sm
    cd = cd_ref[...] * DIFF_SCALE
    sd = sd_ref[...] * DIFF_SCALE
    for j in range(4):
        lo = j * 128
        dq_o[:, lo:lo + 128] = (proj[:, _N_DQ + lo:_N_DQ + lo + 128] * cd
                                + proj[:, _N_DQP + lo:_N_DQP + lo + 128] * sd)
    mq_o[...] = proj[:, _N_MQ:_N_MQ + 512] * MEM_SCALE


def _row_spec(tm, w):
    return pl.BlockSpec((tm, w), lambda i: (i, 0))


def _col_spec(h, tm):
    return pl.BlockSpec((h, tm), lambda i: (0, i))


def _pre_prompt(x, p, tabs):
    rows = x.shape[0]
    tm = TK
    nkb = rows // tm
    in_specs = [
        _row_spec(tm, D_MODEL), _full((1, D_MODEL)), _full(p['wn_prompt'].shape),
        _full((1, MLA_Q_RANK)), _full((1, MLA_KV_RANK)),
        _row_spec(tm, 32), _row_spec(tm, 32), _row_spec(tm, 128), _row_spec(tm, 128),
        _row_spec(tm, 128), _row_spec(tm, 128),
        _col_spec(128, tm), _col_spec(128, tm), _col_spec(128, tm), _col_spec(128, tm),
        _full(p['wt'].shape), _full(p['wuqt'].shape), _full(p['wuqpt'].shape),
        _full(p['wuk_pad'].shape), _full(p['wuvt'].shape),
    ]
    vt_spec = pl.BlockSpec((8, 1, V_ROWS, tm), lambda i: (0, i, 0, 0))
    out_specs = [
        _row_spec(tm, 256), _row_spec(tm, 32), _row_spec(tm, 512), _row_spec(tm, 512),
        _row_spec(tm, 512), _row_spec(tm, 1024), vt_spec,
        _col_spec(1024, tm), _col_spec(512, tm), _col_spec(512, tm), vt_spec,
    ]
    out_shape = [
        jax.ShapeDtypeStruct((rows, 256), F32), jax.ShapeDtypeStruct((rows, 32), F32),
        jax.ShapeDtypeStruct((rows, 512), F32), jax.ShapeDtypeStruct((rows, 512), F32),
        jax.ShapeDtypeStruct((rows, 512), BF16), jax.ShapeDtypeStruct((rows, 1024), BF16),
        jax.ShapeDtypeStruct((8, nkb, V_ROWS, tm), BF16),
        jax.ShapeDtypeStruct((1024, rows), BF16), jax.ShapeDtypeStruct((512, rows), BF16),
        jax.ShapeDtypeStruct((512, rows), BF16),
        jax.ShapeDtypeStruct((8, nkb, V_ROWS, tm), BF16),
    ]
    return pl.pallas_call(
        _pre_prompt_kernel,
        grid=(rows // tm,),
        in_specs=in_specs, out_specs=out_specs, out_shape=out_shape,
        compiler_params=pltpu.CompilerParams(dimension_semantics=("parallel",),
                                             vmem_limit_bytes=VMEM_LIMIT),
        name="pre_prompt",
    )(x, p['pre_mix_g'], p['wn_prompt'], p['gq'], p['gkv'],
      tabs['c32'], tabs['s32'], tabs['cd'], tabs['sd'], tabs['cm'], tabs['sm'],
      tabs['cmt'], tabs['smt'], tabs['cdt'], tabs['sdt'],
      p['wt'], p['wuqt'], p['wuqpt'], p['wuk_pad'], p['wuvt'])


def _pre_sample(x, p, tabs):
    rows = x.shape[0]
    tm = 256
    in_specs = [
        _row_spec(tm, D_MODEL), _full((1, D_MODEL)), _full(p['wn_sample'].shape),
        _full((1, MLA_Q_RANK)), _full((1, MLA_KV_RANK)),
        _row_spec(tm, 32), _row_spec(tm, 32), _row_spec(tm, 128), _row_spec(tm, 128),
        _row_spec(tm, 128), _row_spec(tm, 128),
        _full(p['wuq_pad'].shape), _full(p['wuqp_pad'].shape),
    ]
    out_specs = [
        _row_spec(tm, 256), _row_spec(tm, 32), _row_spec(tm, 512), _row_spec(tm, 512),
        _row_spec(tm, 1024), _row_spec(tm, 512), _row_spec(tm, 512),
    ]
    out_shape = [
        jax.ShapeDtypeStruct((rows, 256), F32), jax.ShapeDtypeStruct((rows, 32), F32),
        jax.ShapeDtypeStruct((rows, 512), F32), jax.ShapeDtypeStruct((rows, 512), F32),
        jax.ShapeDtypeStruct((rows, 1024), F32), jax.ShapeDtypeStruct((rows, 512), F32),
        jax.ShapeDtypeStruct((rows, 512), F32),
    ]
    return pl.pallas_call(
        _pre_sample_kernel,
        grid=(rows // tm,),
        in_specs=in_specs, out_specs=out_specs, out_shape=out_shape,
        compiler_params=pltpu.CompilerParams(dimension_semantics=("parallel",),
                                             vmem_limit_bytes=VMEM_LIMIT),
        name="pre_sample",
    )(x, p['pre_mix_g'], p['wn_sample'], p['gq'], p['gkv'],
      tabs['c32'], tabs['s32'], tabs['cd'], tabs['sd'], tabs['cm'], tabs['sm'],
      p['wuq_pad'], p['wuqp_pad'])


def _chunk_mask(i, j, reps):
    kc = (j * TK + lax.broadcasted_iota(jnp.int32, (TK, TQ), 0)) // CHUNK
    qc = (i * TQ + lax.broadcasted_iota(jnp.int32, (TK, TQ), 1)) // CHUNK
    mask = kc <= qc
    return mask if reps == 1 else jnp.concatenate([mask] * reps, axis=1)


def _flash_blocks(i, heads, pv_lag, qk, softmax, pv):
    def run(steps, masked):
        pending = []
        for j_next, slot_next, j, slot in steps:
            for h in range(heads):
                if j_next is not None:
                    qk(j_next, slot_next, h)
                pending.append((j, h) + softmax(j, slot, h, masked))
                if len(pending) > pv_lag:
                    pv(*pending.pop(0))
        for item in pending:
            pv(*item)

    for h in range(heads):
        qk(0, 0, h)

    def pair(j):
        return [(j + 1, 1, j, 0), (j + 2, 0, j + 1, 1)]

    def body4(jj, c):
        run(pair(4 * jj) + pair(4 * jj + 2), False)
        return c

    def body2(jj, c):
        run(pair(2 * (i - 1)), False)
        return c

    lax.fori_loop(0, i // 2, body4, 0)
    lax.fori_loop(0, i % 2, body2, 0)
    run([(2 * i + 1, 1, 2 * i, 0), (None, None, 2 * i + 1, 1)], True)


def _online_softmax(st, h, m_ref):
    m_old = m_ref[h]
    m_new = jnp.maximum(m_old, jnp.max(st, axis=0, keepdims=True))
    m_ref[h] = m_new
    return jnp.exp2(m_old - m_new), jnp.exp2(st - m_new).astype(BF16)


def _online_pv(j, h, alpha, p, acc_ref, vt_ref):
    acc_ref[h] = alpha * acc_ref[h] + _dot(vt_ref[h, j], p)


def _kblock(j):
    return pl.ds(pl.multiple_of(j * TK, TK), TK)


def _attn_mla_kernel(qt_ref, k_ref, vt_ref, o_ref, s_ref, m_ref, acc_ref):
    i = pl.program_id(1)
    heads = qt_ref.shape[0] // HEAD_PAD
    m_ref[...] = jnp.full(m_ref.shape, NEG_INF, F32)
    acc_ref[...] = jnp.zeros(acc_ref.shape, F32)

    def qk(j, slot, h):
        hs = slice(h * HEAD_PAD, (h + 1) * HEAD_PAD)
        s_ref[slot, h] = _dot(k_ref[_kblock(j), hs], qt_ref[hs, :])

    def softmax(j, slot, h, masked):
        st = s_ref[slot, h]
        if masked:
            st = jnp.where(_chunk_mask(i, j, 1), st, NEG_INF)
        return _online_softmax(st, h, m_ref)

    def pv(j, h, alpha, p):
        _online_pv(j, h, alpha, p, acc_ref, vt_ref)

    _flash_blocks(i, heads, MLA_PV_LAG, qk, softmax, pv)
    for h in range(heads):
        acc = acc_ref[h]
        o_ref[h * MLA_V:(h + 1) * MLA_V, :] = (
            acc[0:MLA_V, :] / acc[MLA_V:MLA_V + 1, :]).astype(BF16)


def _attn_mla(qmt, km, vmt):
    t = km.shape[0]
    hg = ATTN_HEADS_PER_STEP
    groups = MLA_HEADS // hg
    nq = t // TQ
    resident = pl.Buffered(1)
    return pl.pallas_call(
        _attn_mla_kernel,
        grid=(groups, nq),
        in_specs=[
            pl.BlockSpec((hg * HEAD_PAD, TQ), lambda g, i: (g, i)),
            pl.BlockSpec((t, hg * HEAD_PAD), lambda g, i: (0, g), pipeline_mode=resident),
            pl.BlockSpec((hg, t // TK, V_ROWS, TK), lambda g, i: (g, 0, 0, 0),
                         pipeline_mode=resident),
        ],
        out_specs=pl.BlockSpec((hg * MLA_V, TQ), lambda g, i: (g, i)),
        out_shape=jax.ShapeDtypeStruct((MLA_HEADS * MLA_V, t), BF16),
        scratch_shapes=[pltpu.VMEM((2, hg, TK, TQ), F32), pltpu.VMEM((hg, 1, TQ), F32),
                        pltpu.VMEM((hg, V_ROWS, TQ), F32)],
        compiler_params=pltpu.CompilerParams(dimension_semantics=("parallel", "parallel"),
                                             vmem_limit_bytes=VMEM_LIMIT),
        name="attn_mla",
    )(qmt, km, vmt)


def _attn_diff_kernel(lam_init, qt_ref, k_ref, vt_ref, lq1_ref, lk1_ref, lq2_ref, lk2_ref,
                      gsub_ref, o_ref, wq_ref, s_ref, m_ref, acc_ref):
    i = pl.program_id(1)
    heads = vt_ref.shape[0]
    grp = lax.broadcasted_iota(jnp.int32, (128, TQ), 0) // DIFF_DC
    for pr in range(heads // 2):
        qt = qt_ref[pr * 128:(pr + 1) * 128, :]
        zero = jnp.zeros_like(qt)
        for s in range(4):
            wq_ref[pr, :, s * TQ:(s + 1) * TQ] = jnp.where(grp == s, qt, zero)
    m_ref[...] = jnp.full(m_ref.shape, NEG_INF, F32)
    acc_ref[...] = jnp.zeros(acc_ref.shape, F32)

    def qk(j, slot, h):
        pr, hh = divmod(h, 2)
        k = k_ref[_kblock(j), pr * 128:(pr + 1) * 128]
        s_ref[slot, h] = _dot(k, wq_ref[pr, :, hh * 2 * TQ:(hh + 1) * 2 * TQ])

    def softmax(j, slot, h, masked):
        st = s_ref[slot, h]
        if masked:
            st = jnp.where(_chunk_mask(i, j, 2), st, NEG_INF)
        return _online_softmax(st, h, m_ref)

    def pv(j, h, alpha, p):
        _online_pv(j, h, alpha, p, acc_ref, vt_ref)

    _flash_blocks(i, heads, DIFF_PV_LAG, qk, softmax, pv)

    lam = (jnp.exp(jnp.sum(lq1_ref[...] * lk1_ref[...], axis=-1, keepdims=True))
           - jnp.exp(jnp.sum(lq2_ref[...] * lk2_ref[...], axis=-1, keepdims=True))
           + lam_init)
    g = gsub_ref[...] * (1.0 - lam_init)
    for h in range(heads):
        acc = acc_ref[h]
        o0 = acc[0:DIFF_V, 0:TQ] / acc[DIFF_V:DIFF_V + 1, 0:TQ]
        o1 = acc[0:DIFF_V, TQ:2 * TQ] / acc[DIFF_V:DIFF_V + 1, TQ:2 * TQ]
        o = o0 - lam * o1
        o = o * lax.rsqrt(jnp.mean(o * o, axis=0, keepdims=True) + EPS) * g
        o_ref[h * DIFF_V:(h + 1) * DIFF_V, :] = o.astype(BF16)


def _attn_diff(qdt, kd, vdt, lq1, lk1, lq2, lk2, gsub, lam_init):
    t = kd.shape[0]
    nq = t // TQ
    hg = ATTN_HEADS_PER_STEP
    groups = DIFF_HEADS // hg
    vec = _full((1, DIFF_DC))
    resident = pl.Buffered(1)
    return pl.pallas_call(
        functools.partial(_attn_diff_kernel, lam_init),
        grid=(groups, nq),
        in_specs=[
            pl.BlockSpec((hg * DIFF_V, TQ), lambda g, i: (g, i)),
            pl.BlockSpec((t, hg * DIFF_V), lambda g, i: (0, g), pipeline_mode=resident),
            pl.BlockSpec((hg, t // TK, V_ROWS, TK), lambda g, i: (g, 0, 0, 0),
                         pipeline_mode=resident),
            vec, vec, vec, vec, _full((DIFF_V, 1)),
        ],
        out_specs=pl.BlockSpec((hg * DIFF_V, TQ), lambda g, i: (g, i)),
        out_shape=jax.ShapeDtypeStruct((DIFF_HEADS * DIFF_V, t), BF16),
        scratch_shapes=[pltpu.VMEM((hg // 2, 128, 4 * TQ), BF16),
                        pltpu.VMEM((2, hg, TK, 2 * TQ), F32),
                        pltpu.VMEM((hg, 1, 2 * TQ), F32),
                        pltpu.VMEM((hg, V_ROWS, 2 * TQ), F32)],
        compiler_params=pltpu.CompilerParams(dimension_semantics=("parallel", "parallel"),
                                             vmem_limit_bytes=VMEM_LIMIT),
        name="attn_diff",
    )(qdt, kd, vdt, lq1, lk1, lq2, lk2, gsub)


def _attn_mem_kernel(qt_ref, k_ref, vt_ref, o_ref):
    for h in range(MEM_HEADS):
        sl = slice(h * MEM_DH, (h + 1) * MEM_DH)
        st = _dot(k_ref[:, sl], qt_ref[sl, :])
        e = jnp.exp(st - jnp.max(st, axis=0, keepdims=True))
        p = (e / jnp.sum(e, axis=0, keepdims=True)).astype(BF16)
        o_ref[sl, :] = _dot(vt_ref[sl, :], p).astype(BF16)


def _attn_mem(qct, mk, mvt):
    w, t = qct.shape
    tq = 512
    return pl.pallas_call(
        _attn_mem_kernel,
        grid=(t // tq,),
        in_specs=[pl.BlockSpec((w, tq), lambda i: (0, i)), _full(mk.shape), _full(mvt.shape)],
        out_specs=pl.BlockSpec((w, tq), lambda i: (0, i)),
        out_shape=jax.ShapeDtypeStruct((w, t), BF16),
        compiler_params=pltpu.CompilerParams(dimension_semantics=("parallel",)),
        name="attn_mem",
    )(qct, mk, mvt)


def _softmax_two(s_a, s_b):
    m = jnp.maximum(jnp.max(s_a, axis=-1, keepdims=True), jnp.max(s_b, axis=-1, keepdims=True))
    e_a = jnp.exp(s_a - m)
    e_b = jnp.exp(s_b - m)
    inv = 1.0 / (jnp.sum(e_a, axis=-1, keepdims=True) + jnp.sum(e_b, axis=-1, keepdims=True))
    return e_a * inv, e_b * inv


def _diag_blocks(o_all, rows, width, nblk):
    lane_blk = lax.broadcasted_iota(jnp.int32, (rows, nblk * width), 1) // width
    out = jnp.zeros((rows, nblk * width), F32)
    for b in range(nblk):
        out = out + jnp.where(lane_blk == b, o_all[b * rows:(b + 1) * rows, :], 0.0)
    return out


def _sample_mla_kernel(qm_ref, ckvn_ref, krn_ref, mq_ref, ckvp_ref, krp_ref, mk_ref, mv_ref,
                       wabs_ref, wuv_ref, oa_ref, oc_ref):
    nq = qm_ref.shape[0]
    qm = qm_ref[...].astype(BF16)
    qext = jnp.concatenate(
        [_dot(qm[:, h * HEAD_PAD:(h + 1) * HEAD_PAD], wabs_ref[h]) for h in range(MLA_HEADS)],
        axis=0).astype(BF16)
    q_lat = qext[:, 0:MLA_KV_RANK]
    q_rope = qext[:, MLA_KV_RANK:MLA_KV_RANK + MLA_ROPE]
    ckv_p = ckvp_ref[...].astype(BF16)
    ckv_n = ckvn_ref[...].astype(BF16)
    s_p = _dot_nt(q_lat, ckv_p) + _dot_nt(q_rope, krp_ref[...].astype(BF16))
    s_n = _dot_nt(q_lat, ckv_n) + _dot_nt(q_rope, krn_ref[...].astype(BF16))
    p_p, p_n = _softmax_two(s_p, s_n)
    o_lat = _dot(p_p.astype(BF16), ckv_p) + _dot(p_n.astype(BF16), ckv_n)
    o_all = _dot(o_lat.astype(BF16), wuv_ref[...])
    oa_ref[...] = _diag_blocks(o_all, nq, MLA_V, MLA_HEADS)
    mq = mq_ref[...].astype(BF16)
    for h in range(MEM_HEADS):
        sl = slice(h * MEM_DH, (h + 1) * MEM_DH)
        s = _dot_nt(mq[:, sl], mk_ref[:, sl].astype(BF16))
        e = jnp.exp(s - jnp.max(s, axis=-1, keepdims=True))
        p = (e / jnp.sum(e, axis=-1, keepdims=True)).astype(BF16)
        oc_ref[:, sl] = _dot(p, mv_ref[:, sl].astype(BF16))


def _sample_mla(qm, ckv_new, kr_new, mq, ckv_past, kr_past, mem_k, mem_v, wabs, wuv, nb, nq):
    def rows(w):
        return pl.BlockSpec((nq, w), lambda b: (b, 0))

    def cache(shape):
        return pl.BlockSpec((None,) + shape, lambda b: (b, 0, 0))

    past = ckv_past.shape[1]
    return pl.pallas_call(
        _sample_mla_kernel,
        grid=(nb,),
        in_specs=[rows(1024), rows(256), rows(32), rows(512),
                  cache((past, 256)), cache((past, 32)), cache(mem_k.shape[1:]),
                  cache(mem_v.shape[1:]), _full(wabs.shape), _full(wuv.shape)],
        out_specs=[rows(512), rows(512)],
        out_shape=[jax.ShapeDtypeStruct((nb * nq, 512), F32)] * 2,
        compiler_params=pltpu.CompilerParams(dimension_semantics=("parallel",),
                                             vmem_limit_bytes=VMEM_LIMIT),
        name="sample_mla_mem",
    )(qm, ckv_new, kr_new, mq, ckv_past, kr_past, mem_k, mem_v, wabs, wuv)


def _sample_diff_kernel(lam_init, dq_ref, dkn_ref, dvn_ref, dkp_ref, dvp_ref,
                        lq1_ref, lk1_ref, lq2_ref, lk2_ref, o_ref):
    nq = dq_ref.shape[0]
    w = dq_ref.shape[1]
    nmaps = w // DIFF_DC
    q = dq_ref[...]
    row_grp = lax.broadcasted_iota(jnp.int32, (nmaps * nq, w), 0) // nq
    lane_grp = lax.broadcasted_iota(jnp.int32, (nmaps * nq, w), 1) // DIFF_DC
    qbd = jnp.where(row_grp == lane_grp, jnp.concatenate([q] * nmaps, axis=0), 0.0).astype(BF16)
    k_p = dkp_ref[...].astype(BF16)
    k_n = dkn_ref[...].astype(BF16)
    p_p, p_n = _softmax_two(_dot_nt(qbd, k_p), _dot_nt(qbd, k_n))
    lam = (jnp.exp(jnp.sum(lq1_ref[...] * lk1_ref[...], axis=-1, keepdims=True))
           - jnp.exp(jnp.sum(lq2_ref[...] * lk2_ref[...], axis=-1, keepdims=True))
           + lam_init)

    def combine(p):
        parts = [p[(2 * h) * nq:(2 * h + 1) * nq, :] - lam * p[(2 * h + 1) * nq:(2 * h + 2) * nq, :]
                 for h in range(nmaps // 2)]
        return jnp.concatenate(parts, axis=0).astype(BF16)

    o_all = (_dot(combine(p_p), dvp_ref[...].astype(BF16))
             + _dot(combine(p_n), dvn_ref[...].astype(BF16)))
    o_ref[...] = _diag_blocks(o_all, nq, DIFF_V, nmaps // 2)


def _sample_diff(dq, dk_new, dv_new, dk_past, dv_past, lq1, lk1, lq2, lk2, lam_init, nb, nq):
    hw = 256
    past = dk_past.shape[1]
    rows = pl.BlockSpec((nq, hw), lambda b, g: (b, g))
    cache = pl.BlockSpec((None, past, hw), lambda b, g: (b, 0, g))
    vec = _full((1, DIFF_DC))
    return pl.pallas_call(
        functools.partial(_sample_diff_kernel, lam_init),
        grid=(nb, 512 // hw),
        in_specs=[rows, rows, rows, cache, cache, vec, vec, vec, vec],
        out_specs=rows,
        out_shape=jax.ShapeDtypeStruct((nb * nq, 512), F32),
        compiler_params=pltpu.CompilerParams(dimension_semantics=("parallel", "parallel"),
                                             vmem_limit_bytes=VMEM_LIMIT),
        name="sample_diff",
    )(dq, dk_new, dv_new, dk_past, dv_past, lq1, lk1, lq2, lk2)


def _subln_kernel(lam_init, o_ref, g_ref, out_ref):
    g = g_ref[...] * (1.0 - lam_init)
    for h in range(DIFF_HEADS):
        o = o_ref[h * DIFF_V:(h + 1) * DIFF_V, :]
        out_ref[h * DIFF_V:(h + 1) * DIFF_V, :] = (
            o * lax.rsqrt(jnp.mean(o * o, axis=0, keepdims=True) + EPS) * g).astype(BF16)


def _subln(ot, gsub, lam_init):
    return pl.pallas_call(
        functools.partial(_subln_kernel, lam_init),
        out_shape=jax.ShapeDtypeStruct(ot.shape, BF16),
        name="sample_subln",
    )(ot, gsub)


def _mix_kernel(x_ref, oa_ref, ob_ref, oc_ref, g_ref, wg_ref, bg_ref, woa_ref, wob_ref, woc_ref,
                wout_ref, gpost_ref, y_ref):
    x = x_ref[...]
    xn = _rms(x, g_ref[...]).astype(BF16)
    d = x.shape[1]
    merged = None
    for b, (o_ref, w_ref) in enumerate(((oa_ref, woa_ref), (ob_ref, wob_ref), (oc_ref, woc_ref))):
        gate = jax.nn.sigmoid(_dot(xn, wg_ref[:, b * d:(b + 1) * d]) + bg_ref[:, b * d:(b + 1) * d])
        term = gate * _dot_tn(o_ref[...], w_ref[...])
        merged = term if merged is None else merged + term
    mix = _dot(merged.astype(BF16), wout_ref[...])
    y_ref[...] = x + _rms(mix, gpost_ref[...])


def _mix(x, oat, obt, oct, p):
    rows, d = x.shape
    tm = 512 if rows % 512 == 0 else rows
    ot_spec = pl.BlockSpec((512, tm), lambda i: (0, i))
    return pl.pallas_call(
        _mix_kernel,
        grid=(rows // tm,),
        in_specs=[_row_spec(tm, d), ot_spec, ot_spec, ot_spec, _full((1, d)),
                  _full(p['w_gate'].shape), _full((1, 3 * d)), _full((512, d)), _full((512, d)),
                  _full((512, d)), _full((d, d)), _full((1, d))],
        out_specs=_row_spec(tm, d),
        out_shape=jax.ShapeDtypeStruct((rows, d), F32),
        compiler_params=pltpu.CompilerParams(dimension_semantics=("parallel",),
                                             vmem_limit_bytes=VMEM_LIMIT),
        name="mix",
    )(x, oat, obt, oct, p['pre_mix_g'], p['w_gate'], p['b_gate'], p['w_o_mla'], p['w_o_diff'],
      p['w_o_mem'], p['w_out'], p['post_mix_g'])


def _mlp_kernel(x_ref, g_ref, wup_ref, wdn_ref, gpost_ref, y_ref):
    x = x_ref[...]
    h = _rms(x, g_ref[...]).astype(BF16)
    u = jnp.maximum(_dot(h, wup_ref[...]), 0.0)
    f = _dot((u * u).astype(BF16), wdn_ref[...])
    y_ref[...] = x + _rms(f, gpost_ref[...])


def _mlp(x, p):
    rows, d = x.shape
    tm = 256
    return pl.pallas_call(
        _mlp_kernel,
        grid=(rows // tm,),
        in_specs=[_row_spec(tm, d), _full((1, d)), _full(p['w_mlp_up'].shape),
                  _full(p['w_mlp_down'].shape), _full((1, d))],
        out_specs=_row_spec(tm, d),
        out_shape=jax.ShapeDtypeStruct((rows, d), F32),
        compiler_params=pltpu.CompilerParams(dimension_semantics=("parallel",),
                                             vmem_limit_bytes=VMEM_LIMIT),
        name="mlp",
    )(x, p['pre_mlp_g'], p['w_mlp_up'], p['w_mlp_down'], p['post_mlp_g'])


def _post_kernel(x_ref, oa_ref, ob_ref, oc_ref, g_ref, wg_ref, bg_ref, woa_ref, wob_ref, woc_ref,
                 wout_ref, gpost_ref, gmlp_ref, wup_ref, wdn_ref, gpostmlp_ref, y_ref, x1_ref):
    _mix_kernel(x_ref, oa_ref, ob_ref, oc_ref, g_ref, wg_ref, bg_ref, woa_ref, wob_ref, woc_ref,
                wout_ref, gpost_ref, x1_ref)
    _mlp_kernel(x1_ref, gmlp_ref, wup_ref, wdn_ref, gpostmlp_ref, y_ref)


def _post(x, oat, obt, oct, p):
    rows, d = x.shape
    tm = 256
    ot_spec = pl.BlockSpec((512, tm), lambda i: (0, i))

    def const(shape):
        nd = len(shape)
        return pl.BlockSpec(shape, lambda i: (0,) * nd, pipeline_mode=pl.Buffered(1))

    return pl.pallas_call(
        _post_kernel,
        grid=(rows // tm,),
        in_specs=[_row_spec(tm, d), ot_spec, ot_spec, ot_spec, const((1, d)),
                  const(p['w_gate'].shape), const((1, 3 * d)), const((512, d)), const((512, d)),
                  const((512, d)), const((d, d)), const((1, d)), const((1, d)),
                  const(p['w_mlp_up'].shape), const(p['w_mlp_down'].shape), const((1, d))],
        out_specs=_row_spec(tm, d),
        out_shape=jax.ShapeDtypeStruct((rows, d), F32),
        scratch_shapes=[pltpu.VMEM((tm, d), F32)],
        compiler_params=pltpu.CompilerParams(dimension_semantics=("parallel",),
                                             vmem_limit_bytes=VMEM_LIMIT),
        name="post",
    )(x, oat, obt, oct, p['pre_mix_g'], p['w_gate'], p['b_gate'], p['w_o_mla'], p['w_o_diff'],
      p['w_o_mem'], p['w_out'], p['post_mix_g'], p['pre_mlp_g'], p['w_mlp_up'], p['w_mlp_down'],
      p['post_mlp_g'])


def _partner(width, group, half):
    idx = np.arange(width)
    sign = np.zeros(width, np.float32)
    d = idx % group
    first = d < half
    second = (d >= half) & (d < 2 * half)
    src = np.where(first, idx + half, np.where(second, idx - half, idx))
    sign[first] = -1.0
    sign[second] = 1.0
    return src, sign


def _take_signed(w, src, sign):
    return w[:, src] * jnp.asarray(sign)[None, :]


def _pad_cols(w, total):
    return jnp.pad(w, ((0, 0), (0, total - w.shape[1])))


def _prep_layer(l, w_in, mla_w_uq, mla_w_uk, mla_w_uv, w_mem_k, w_mem_v, w_o_mla, w_o_diff, w_o_mem,
                w_gate, b_gate, w_out, w_mlp_up, w_mlp_down, gains):
    cq, ckv, kr, dq, dk, dv, mq = jnp.split(
        w_in[l], np.cumsum((384, 256, 32, 512, 512, 512))[...].tolist(), axis=1)
    src_d, sign_d = _partner(512, DIFF_DC, DIFF_ROT // 2)
    src_r, sign_r = _partner(32, 32, MLA_ROPE // 2)
    dkp = _take_signed(dk, src_d, sign_d)
    dqp = _take_signed(dq, src_d, sign_d)
    krp = _take_signed(kr, src_r, sign_r)
    common = [cq, ckv, dk, dkp, dv, _pad_cols(kr, 128), _pad_cols(krp, 128)]
    zeros64 = jnp.zeros((D_MODEL, MLA_NOPE), F32)
    kr128 = _pad_cols(jnp.concatenate([zeros64, kr], axis=1), 128)
    kr128p = _pad_cols(jnp.concatenate([zeros64, krp], axis=1), 128)
    head_w = MLA_NOPE + MLA_ROPE
    e = np.arange(HEAD_PAD)
    valid = e < head_w
    src_q = np.concatenate([h * head_w + np.where(valid, e, 0) for h in range(MLA_HEADS)])
    sign_q = np.tile(valid.astype(np.float32), MLA_HEADS)
    in_rope1 = (e >= MLA_NOPE) & (e < MLA_NOPE + MLA_ROPE // 2)
    in_rope2 = (e >= MLA_NOPE + MLA_ROPE // 2) & valid
    pe = np.where(in_rope1, e + MLA_ROPE // 2, np.where(in_rope2, e - MLA_ROPE // 2, 0))
    src_qp = np.concatenate([h * head_w + pe for h in range(MLA_HEADS)])
    sign_qp = np.tile(np.where(in_rope1, -1.0, np.where(in_rope2, 1.0, 0.0)).astype(np.float32),
                      MLA_HEADS)
    wuq_pad = _take_signed(mla_w_uq[l], src_q, sign_q)
    wuqp_pad = _take_signed(mla_w_uq[l], src_qp, sign_qp)
    wuk = mla_w_uk[l]
    wuk_pad = jnp.pad(wuk, ((0, 0), (0, 0), (0, HEAD_PAD - MLA_NOPE))).reshape(MLA_KV_RANK, -1)
    wuv = mla_w_uv[l].reshape(MLA_KV_RANK, MLA_HEADS * MLA_V)
    sel = np.zeros((HEAD_PAD, 128), np.float32)
    sel[MLA_NOPE + np.arange(MLA_ROPE), np.arange(MLA_ROPE)] = 1.0
    wabs = jnp.concatenate([
        jnp.pad(jnp.transpose(wuk, (1, 2, 0)), ((0, 0), (0, HEAD_PAD - MLA_NOPE), (0, 0))),
        jnp.broadcast_to(jnp.asarray(sel), (MLA_HEADS, HEAD_PAD, 128))], axis=2)
    bf = lambda a: a.astype(BF16)
    row = lambda a: a[l][None, :]
    p = {
        'wn_prompt': bf(jnp.concatenate(common + [kr128, kr128p], axis=1)),
        'wn_sample': bf(jnp.concatenate(common + [dq, dqp, mq], axis=1)),
        'wt': bf(jnp.concatenate([dq, dqp, mq, dv], axis=1).T),
        'wuqt': bf(wuq_pad.T), 'wuqpt': bf(wuqp_pad.T),
        'wuq_pad': bf(wuq_pad), 'wuqp_pad': bf(wuqp_pad),
        'wuk_pad': bf(wuk_pad), 'wuvt': bf(wuv.T), 'wuv': bf(wuv), 'wabs': bf(wabs),
        'w_mem_k': bf(w_mem_k[l]), 'w_mem_v': bf(w_mem_v[l]), 'w_mem_vt': bf(w_mem_v[l].T),
        'w_o_mla': bf(w_o_mla[l]), 'w_o_diff': bf(w_o_diff[l]), 'w_o_mem': bf(w_o_mem[l]),
        'w_gate': bf(w_gate[l]), 'b_gate': row(b_gate), 'w_out': bf(w_out[l]),
        'w_mlp_up': bf(w_mlp_up[l]), 'w_mlp_down': bf(w_mlp_down[l]),
    }
    for name, g in gains.items():
        p[name] = row(g)
    return p


def _rope_cos_sin(pos, rot_dim, theta):
    half = rot_dim // 2
    inv = jnp.power(jnp.float32(theta), -jnp.arange(half, dtype=F32) * (2.0 / rot_dim))
    ang = pos.astype(F32)[:, None] * inv[None, :]
    return jnp.cos(ang), jnp.sin(ang)


def _tables(pos, reps):
    n = pos.shape[0]
    cm, sm = _rope_cos_sin(pos, MLA_ROPE, MLA_THETA)
    cd, sd = _rope_cos_sin(pos, DIFF_ROT, ROPE_THETA)
    one = lambda w: jnp.ones((n, w), F32)
    zero = lambda w: jnp.zeros((n, w), F32)
    t = {
        'c32': jnp.concatenate([cm, cm], axis=1), 's32': jnp.concatenate([sm, sm], axis=1),
        'cm': jnp.concatenate([one(64), cm, cm, one(32)], axis=1),
        'sm': jnp.concatenate([zero(64), sm, sm, zero(32)], axis=1),
        'cd': jnp.tile(jnp.concatenate([cd, cd, one(24)], axis=1), (1, 4)),
        'sd': jnp.tile(jnp.concatenate([sd, sd, zero(24)], axis=1), (1, 4)),
    }
    return {k: jnp.tile(v, (reps, 1)) for k, v in t.items()}


def kernel(x_prompt, x_sample, cache_mla_ckv, cache_mla_krope, cache_diff_k, cache_diff_v, cache_mem_k, cache_mem_v, mem_prompt, pre_mix_g, w_in, mla_q_norm_g, mla_w_uq, mla_kv_norm_g, mla_w_uk, mla_w_uv, diff_lq1, diff_lk1, diff_lq2, diff_lk2, diff_subln_g, mem_norm_g, w_mem_k, w_mem_v, w_o_mla, w_o_diff, w_o_mem, w_gate, b_gate, w_out, post_mix_g, pre_mlp_g, w_mlp_up, w_mlp_down, post_mlp_g):
    depth = w_in.shape[0]
    bp, t, d = x_prompt.shape
    nb, nq, _ = x_sample.shape
    past = cache_mla_ckv.shape[2]
    assert bp == 1 and t % TQ == 0 and TQ == 2 * TK and TK % CHUNK == 0 and d == D_MODEL
    assert past % CHUNK == 0 and nq <= CHUNK

    tabs_p = _tables(jnp.arange(t, dtype=jnp.int32), 1)
    tabs_p['cmt'] = (tabs_p['cm'] * (MLA_SCALE * LOG2E)).T
    tabs_p['smt'] = (tabs_p['sm'] * (MLA_SCALE * LOG2E)).T
    tabs_p['cdt'] = (tabs_p['cd'] * (DIFF_SCALE * LOG2E)).T
    tabs_p['sdt'] = (tabs_p['sd'] * (DIFF_SCALE * LOG2E)).T
    tabs_s = _tables(past + jnp.arange(nq, dtype=jnp.int32), nb)

    xp = x_prompt.reshape(t, d)
    xs = x_sample.reshape(nb * nq, d)
    outs = {k: [] for k in ('p_ckv', 'p_kr', 'p_dk', 'p_dv', 'p_mk', 'p_mv',
                            's_ckv', 's_kr', 's_dk', 's_dv')}
    for l in range(depth):
        lam_init = 0.8 - 0.6 * math.exp(-0.3 * l)
        p = _prep_layer(l, w_in, mla_w_uq, mla_w_uk, mla_w_uv, w_mem_k, w_mem_v, w_o_mla, w_o_diff,
                        w_o_mem, w_gate, b_gate, w_out, w_mlp_up, w_mlp_down,
                        {'pre_mix_g': pre_mix_g, 'gq': mla_q_norm_g, 'gkv': mla_kv_norm_g,
                         'mem_norm_g': mem_norm_g, 'post_mix_g': post_mix_g,
                         'pre_mlp_g': pre_mlp_g, 'post_mlp_g': post_mlp_g})
        lq1, lk1, lq2, lk2 = (a[l][None, :] for a in (diff_lq1, diff_lk1, diff_lq2, diff_lk2))
        gsub = diff_subln_g[l][:, None]

        mk, mv, mkb, mvt = _memkv(mem_prompt[0], p['mem_norm_g'], p['w_mem_k'], p['w_mem_v'],
                                  p['w_mem_vt'])
        (ckv_p, kr_p, dk_p, dv_p, kd, km, vmt, qmt, qdt, qct, vdt) = _pre_prompt(xp, p, tabs_p)
        oat = _attn_mla(qmt, km, vmt)
        obt = _attn_diff(qdt, kd, vdt, lq1, lk1, lq2, lk2, gsub, lam_init)
        oct = _attn_mem(qct, mkb, mvt)
        xp = _post(xp, oat, obt, oct, p)

        (ckv_s, kr_s, dk_s, dv_s, qm_s, dq_s, mq_s) = _pre_sample(xs, p, tabs_s)
        oa_s, oc_s = _sample_mla(qm_s, ckv_s, kr_s, mq_s, cache_mla_ckv[l], cache_mla_krope[l],
                                 cache_mem_k[l].reshape(nb, -1, 512),
                                 cache_mem_v[l].reshape(nb, -1, 512), p['wabs'], p['wuv'], nb, nq)
        ob_s = _sample_diff(dq_s, dk_s, dv_s, cache_diff_k[l].reshape(nb, past, 512),
                            cache_diff_v[l].reshape(nb, past, 512), lq1, lk1, lq2, lk2,
                            lam_init, nb, nq)
        obt_s = _subln(ob_s.T, gsub, lam_init)
        xs = _post(xs, oa_s.T.astype(BF16), obt_s, oc_s.T.astype(BF16), p)

        outs['p_ckv'].append(ckv_p.reshape(1, t, MLA_KV_RANK))
        outs['p_kr'].append(kr_p.reshape(1, t, MLA_ROPE))
        outs['p_dk'].append(dk_p.reshape(1, t, DIFF_HEADS, DIFF_V))
        outs['p_dv'].append(dv_p.reshape(1, t, DIFF_HEADS, DIFF_V))
        outs['p_mk'].append(mk.reshape(1, -1, MEM_HEADS, MEM_DH))
        outs['p_mv'].append(mv.reshape(1, -1, MEM_HEADS, MEM_DH))
        outs['s_ckv'].append(ckv_s.reshape(nb, nq, MLA_KV_RANK))
        outs['s_kr'].append(kr_s.reshape(nb, nq, MLA_ROPE))
        outs['s_dk'].append(dk_s.reshape(nb, nq, DIFF_HEADS, DIFF_V))
        outs['s_dv'].append(dv_s.reshape(nb, nq, DIFF_HEADS, DIFF_V))

    st = lambda k: jnp.stack(outs[k], axis=0)
    return (xp.reshape(1, t, d), xs.reshape(nb, nq, d),
            st('p_ckv'), st('p_kr'), st('p_dk'), st('p_dv'), st('p_mk'), st('p_mv'),
            st('s_ckv'), st('s_kr'), st('s_dk'), st('s_dv'))
```
